```python
import jax, jax.numpy as jnp
from jax import lax
import numpy as np

D_MODEL = 1024
BATCH = 8
SEQ = 4096
DEPTH = 1

GRID_W = 64
ATTN_HEADS = 8
ATTN_KV_HEADS = 2
ATTN_HEAD_DIM = 64
RET_HEADS = 4
RET_KEY_DIM = 64
RET_VALUE_DIM = 128
Q_BLOCK = 128
RET_CHUNK = 128
ROPE_THETA = 10000.0
EPS = 1e-6

ATTN_WIDTH = ATTN_HEADS * ATTN_HEAD_DIM
ATTN_KV_WIDTH = ATTN_KV_HEADS * ATTN_HEAD_DIM
RET_QK_WIDTH = RET_HEADS * RET_KEY_DIM
RET_V_WIDTH = RET_HEADS * RET_VALUE_DIM
MERGE_WIDTH = 2 * D_MODEL
IN_SIZES = (ATTN_WIDTH, ATTN_KV_WIDTH, ATTN_KV_WIDTH, ATTN_WIDTH,
            RET_QK_WIDTH, RET_QK_WIDTH, RET_V_WIDTH, RET_V_WIDTH, MERGE_WIDTH)
IN_WIDTH = int(sum(IN_SIZES))
IN_SPLITS = tuple(int(s) for s in np.cumsum(IN_SIZES)[:-1])

kernel_name = "hybrid_gqa_axialrope_bidir_retention_gated_merge"


def rmsnorm(x, g):
    xf = x.astype(jnp.float32)
    xf = xf * lax.rsqrt(jnp.mean(xf * xf, axis=-1, keepdims=True) + EPS)
    return xf.astype(x.dtype) * g


def axial_rope_tables(seq_len, head_dim):
    rows = seq_len // GRID_W
    r, cidx = jnp.meshgrid(jnp.arange(rows), jnp.arange(GRID_W), indexing='ij')
    row = r.reshape(-1).astype(jnp.float32)
    col = cidx.reshape(-1).astype(jnp.float32)
    half = head_dim // 2
    inv_freq = ROPE_THETA ** (-jnp.arange(0, half, 2, dtype=jnp.float32) / half)
    ang_r = row[:, None] * inv_freq[None, :]
    ang_c = col[:, None] * inv_freq[None, :]
    return (jnp.cos(ang_r), jnp.sin(ang_r), jnp.cos(ang_c), jnp.sin(ang_c))


def _rotate(xp, cos, sin):
    x1, x2 = jnp.split(xp, 2, axis=-1)
    cos = cos[None, :, None, :].astype(xp.dtype)
    sin = sin[None, :, None, :].astype(xp.dtype)
    return jnp.concatenate([x1 * cos - x2 * sin, x2 * cos + x1 * sin], axis=-1)


def apply_axial_rope(x, tables):
    cos_r, sin_r, cos_c, sin_c = tables
    xr, xc = jnp.split(x, 2, axis=-1)
    return jnp.concatenate([_rotate(xr, cos_r, sin_r), _rotate(xc, cos_c, sin_c)], axis=-1)


def gqa_attention(q, k, v, qn_g, kn_g, tables):
    B, S = q.shape[0], q.shape[1]
    G = ATTN_HEADS // ATTN_KV_HEADS
    q = apply_axial_rope(rmsnorm(q, qn_g), tables) * (ATTN_HEAD_DIM ** -0.5)
    k = apply_axial_rope(rmsnorm(k, kn_g), tables)
    nblk = S // Q_BLOCK
    qb = q.reshape(B, nblk, Q_BLOCK, ATTN_KV_HEADS, G, ATTN_HEAD_DIM).transpose(1, 0, 2, 3, 4, 5)

    def block(qi):
        s = jnp.einsum('bqkgd,bskd->bkgqs', qi, k).astype(jnp.float32)
        p = jax.nn.softmax(s, axis=-1).astype(v.dtype)
        return jnp.einsum('bkgqs,bskd->bqkgd', p, v)

    o = lax.map(block, qb)
    return o.transpose(1, 0, 2, 3, 4, 5).reshape(B, S, ATTN_WIDTH)


def retention_one_direction(q, k, v, log_gamma, strict):
    B, S, H, dk = q.shape
    dv = v.shape[-1]
    C = RET_CHUNK
    N = S // C
    dt = q.dtype
    qc = q.reshape(B, N, C, H, dk)
    kc = k.reshape(B, N, C, H, dk)
    vc = v.reshape(B, N, C, H, dv)
    idx = jnp.arange(C, dtype=jnp.float32)
    diff = idx[:, None] - idx[None, :]
    mask = (diff > 0) if strict else (diff >= 0)
    decay_intra = jnp.where(mask[None], jnp.exp(log_gamma[:, None, None] * jnp.maximum(diff, 0.0)[None]), 0.0)
    scores = jnp.einsum('bnihd,bnjhd->bnhij', qc, kc) * decay_intra.astype(dt)[None, None]
    o_intra = jnp.einsum('bnhij,bnjhe->bnihe', scores, vc)
    k_dec = jnp.exp(log_gamma[None, :] * (C - 1 - idx)[:, None]).astype(dt)
    kv = jnp.einsum('bnjhd,bnjhe->nbhde', kc * k_dec[:, :, None], vc)
    chunk_decay = jnp.exp(log_gamma * C).astype(kv.dtype)[None, :, None, None]

    def step(R, kv_n):
        return chunk_decay * R + kv_n, R

    _, R_prev = lax.scan(step, jnp.zeros_like(kv[0]), kv)
    q_dec = jnp.exp(log_gamma[None, :] * (idx + 1.0)[:, None]).astype(dt)
    o_inter = jnp.einsum('bnihd,nbhde->bnihe', qc * q_dec[:, :, None], R_prev)
    return (o_intra + o_inter).reshape(B, S, H, dv)


def bidirectional_retention(q, k, v, w_dec_f, w_dec_b, gn_g, tables):
    B, S = q.shape[0], q.shape[1]
    q = apply_axial_rope(q, tables)
    k = apply_axial_rope(k, tables) * (RET_KEY_DIM ** -0.5)
    lg_f = jax.nn.log_sigmoid(w_dec_f.astype(jnp.float32))
    lg_b = jax.nn.log_sigmoid(w_dec_b.astype(jnp.float32))
    o_f = retention_one_direction(q, k, v, lg_f, False)
    o_b = jnp.flip(retention_one_direction(jnp.flip(q, 1), jnp.flip(k, 1), jnp.flip(v, 1), lg_b, True), 1)
    o = (o_f + o_b).astype(jnp.float32)
    mu = jnp.mean(o, axis=-1, keepdims=True)
    var = jnp.mean(jnp.square(o - mu), axis=-1, keepdims=True)
    o = ((o - mu) * lax.rsqrt(var + EPS)).astype(v.dtype)
    return o.reshape(B, S, RET_V_WIDTH) * gn_g


def setup_inputs(seed: int = 0) -> dict:
    key = jax.random.key(seed)
    ks = jax.random.split(key, 20)
    f32 = jnp.float32

    def w(k, shape, fan_in):
        return jax.random.normal(k, shape, f32) * (fan_in ** -0.5)

    def gain(k, shape):
        return 1.0 + 0.05 * jax.random.normal(k, shape, f32)

    base = jnp.log(2.0 ** (5.0 + jnp.arange(RET_HEADS, dtype=f32)) - 1.0)
    return {
        "x": jax.random.normal(ks[0], (BATCH, SEQ, D_MODEL), f32),
        "c": jax.random.normal(ks[1], (BATCH, D_MODEL), f32),
        "w_ada": w(ks[2], (DEPTH, D_MODEL, 3 * D_MODEL), D_MODEL) * 0.5,
        "b_ada": 0.02 * jax.random.normal(ks[3], (DEPTH, 3 * D_MODEL), f32),
        "g_pre": gain(ks[4], (DEPTH, D_MODEL)),
        "w_in": w(ks[5], (DEPTH, D_MODEL, IN_WIDTH), D_MODEL),
        "qn_g": gain(ks[6], (DEPTH, ATTN_HEAD_DIM)),
        "kn_g": gain(ks[7], (DEPTH, ATTN_HEAD_DIM)),
        "w_dec_f": base[None] + 0.1 * jax.random.normal(ks[8], (DEPTH, RET_HEADS), f32),
        "w_dec_b": base[None] + 0.1 * jax.random.normal(ks[9], (DEPTH, RET_HEADS), f32),
        "gn_g": gain(ks[10], (DEPTH, RET_V_WIDTH)),
        "w_pa": w(ks[11], (DEPTH, ATTN_WIDTH, D_MODEL), ATTN_WIDTH),
        "w_pr": w(ks[12], (DEPTH, RET_V_WIDTH, D_MODEL), RET_V_WIDTH),
        "w_out": w(ks[13], (DEPTH, D_MODEL, D_MODEL), D_MODEL),
        "g_post": gain(ks[14], (DEPTH, D_MODEL)),
    }


def reference(x, c, w_ada, b_ada, g_pre, w_in, qn_g, kn_g, w_dec_f, w_dec_b, gn_g,
              w_pa, w_pr, w_out, g_post):
    B, S, D = x.shape
    tables = axial_rope_tables(S, ATTN_HEAD_DIM)
    c_act = jax.nn.silu(c)
    for l in range(DEPTH):
        mod = c_act @ w_ada[l] + b_ada[l]
        shift, scale, gate = jnp.split(mod, 3, axis=-1)
        h = rmsnorm(x, g_pre[l]) * (1.0 + scale[:, None, :]) + shift[:, None, :]
        p = h @ w_in[l]
        qa, ka, va, za, qr, kr, vr, zr, gl = jnp.split(p, IN_SPLITS, axis=-1)
        ya = gqa_attention(qa.reshape(B, S, ATTN_HEADS, ATTN_HEAD_DIM),
                           ka.reshape(B, S, ATTN_KV_HEADS, ATTN_HEAD_DIM),
                           va.reshape(B, S, ATTN_KV_HEADS, ATTN_HEAD_DIM),
                           qn_g[l], kn_g[l], tables) * jax.nn.silu(za)
        yr = bidirectional_retention(qr.reshape(B, S, RET_HEADS, RET_KEY_DIM),
                                     kr.reshape(B, S, RET_HEADS, RET_KEY_DIM),
                                     vr.reshape(B, S, RET_HEADS, RET_VALUE_DIM),
                                     w_dec_f[l], w_dec_b[l], gn_g[l], tables) * jax.nn.silu(zr)
        g_att, g_ret = jnp.split(jax.nn.sigmoid(gl), 2, axis=-1)
        merged = g_att * (ya @ w_pa[l]) + g_ret * (yr @ w_pr[l])
        y = rmsnorm(merged @ w_out[l], g_post[l])
        x = x + gate[:, None, :] * y
    return x
```

```python
import functools
import math

import jax
import jax.numpy as jnp
from jax import lax
from jax.experimental import pallas as pl
from jax.experimental.pallas import tpu as pltpu

GRID_W = 64
ATTN_HEADS = 8
ATTN_KV_HEADS = 2
HEAD_DIM = 64
RET_HEADS = 4
RET_VALUE_DIM = 128
ROPE_THETA = 10000.0
EPS = 1e-6

LANES = 128
VMEM_LIMIT_BYTES = 56 * 1024 * 1024

F32 = jnp.float32
BF16 = jnp.bfloat16
LOG2E = math.log2(math.e)


def _mm(a, b):
    return jnp.dot(a, b, preferred_element_type=F32)


def _mm_nt(a, b):
    return lax.dot_general(a, b, (((1,), (1,)), ((), ())), preferred_element_type=F32)


def _mm_tn(a, b):
    return lax.dot_general(a, b, (((0,), (0,)), ((), ())), preferred_element_type=F32)


def _sigmoid(x):
    return 1.0 / (1.0 + jnp.exp(-x))


def _params(*semantics):
    return pltpu.CompilerParams(dimension_semantics=semantics, vmem_limit_bytes=VMEM_LIMIT_BYTES)


def _ada_kernel(c_ref, w_ref, b_ref, o_ref):
    c = c_ref[...]
    ca = (c * _sigmoid(c)).astype(BF16)
    o_ref[...] = _mm(ca, w_ref[...].astype(BF16)) + b_ref[...]


def _ada(c, w, b):
    bsz, d = c.shape
    n = w.shape[1]
    return pl.pallas_call(
        _ada_kernel,
        grid=(n // d,),
        in_specs=[pl.BlockSpec((bsz, d), lambda j: (0, 0)),
                  pl.BlockSpec((d, d), lambda j: (0, j)),
                  pl.BlockSpec((1, d), lambda j: (0, j))],
        out_specs=pl.BlockSpec((bsz, d), lambda j: (0, j)),
        out_shape=jax.ShapeDtypeStruct((bsz, n), F32),
        compiler_params=_params("arbitrary"),
        name="ada_mod",
    )(c, w, b.reshape(1, n))


def _rope(x, cos, sin, lo):
    partner = jnp.where(lo, pltpu.roll(x, LANES - 16, 1), pltpu.roll(x, 16, 1))
    return x * cos + partner * sin


def _head_rms(x, ones_blk):
    sq = x * x
    hi = sq.astype(BF16)
    lo = (sq - hi.astype(F32)).astype(BF16)
    ms = (_mm(hi, ones_blk) + _mm(lo, ones_blk)) * (1.0 / HEAD_DIM)
    return x * lax.rsqrt(ms + EPS)


def _inproj_kernel(x_ref, mul_ref, shift_ref, w_ref, cos_ref, sin_ref, ones_ref, qg_ref, kg_ref,
                   qa_ref, ka_ref, va_ref, za_ref, qr_ref, kr_ref, vr_ref, zr_ref, gl_ref, *, tm):
    x = x_ref[0]
    ms = jnp.mean(x * x, axis=-1, keepdims=True)
    h = (x * lax.rsqrt(ms + EPS) * mul_ref[0] + shift_ref[0]).astype(BF16)

    cos = cos_ref[...]
    sin = sin_ref[...]
    lane = lax.broadcasted_iota(jnp.int32, (tm, LANES), 1)
    lo = (lane & 31) < 16
    ones_blk = ones_ref[...]

    def rope_cols(y, scale_row):
        outs = []
        for j in range(y.shape[1] // LANES):
            blk = y[:, j * LANES:(j + 1) * LANES]
            if scale_row is not None:
                blk = blk * scale_row[:, j * LANES:(j + 1) * LANES]
            outs.append(_rope(blk, cos, sin, lo))
        return outs

    col = 0

    def seg(width):
        nonlocal col
        acc = _mm(h, w_ref[:, col:col + width])
        col += width
        return acc

    q = _head_rms(seg(512), ones_blk)
    for j, blk in enumerate(rope_cols(q, qg_ref[...])):
        blk = blk.astype(BF16)
        qa_ref[0, 2 * j] = blk[:, :HEAD_DIM]
        qa_ref[0, 2 * j + 1] = blk[:, HEAD_DIM:]
    k = _head_rms(seg(128), ones_blk[:LANES, :LANES])
    blk = rope_cols(k, kg_ref[...])[0].astype(BF16)
    ka_ref[0, 0] = blk[:, :HEAD_DIM]
    ka_ref[0, 1] = blk[:, HEAD_DIM:]
    v = seg(256)
    for g in range(ATTN_KV_HEADS):
        blk = v[:, g * LANES:(g + 1) * LANES]
        va_ref[0, g] = jnp.where(lane == HEAD_DIM, 1.0, blk).astype(BF16)
    z = seg(512)
    za_ref[0] = (z * _sigmoid(z)).astype(BF16)
    for ref, scale in ((qr_ref, None), (kr_ref, HEAD_DIM ** -0.5)):
        for j, blk in enumerate(rope_cols(seg(256), None)):
            if scale is not None:
                blk = blk * scale
            blk = blk.astype(BF16)
            ref[0, 2 * j] = blk[:, :HEAD_DIM]
            ref[0, 2 * j + 1] = blk[:, HEAD_DIM:]
    vr_ref[0] = seg(512).astype(BF16)
    z = seg(512)
    zr_ref[0] = (z * _sigmoid(z)).astype(BF16)
    for j in range(4):
        gl_ref[0, :, j * 512:(j + 1) * 512] = _sigmoid(seg(512)).astype(BF16)


def _inproj(x, mul, shift, w_aug, cos, sin, ones_blk, qg, kg, *, tm):
    bsz, s, d = x.shape
    n = w_aug.shape[1]
    nt = s // tm
    const = lambda b, i: (0, 0)
    tok3 = lambda b, i: (b, i, 0)
    head4 = lambda b, i: (b, 0, i, 0)
    out_shape = [
        jax.ShapeDtypeStruct((bsz, ATTN_HEADS, s, HEAD_DIM), BF16),
        jax.ShapeDtypeStruct((bsz, ATTN_KV_HEADS, s, HEAD_DIM), BF16),
        jax.ShapeDtypeStruct((bsz, ATTN_KV_HEADS, s, LANES), BF16),
        jax.ShapeDtypeStruct((bsz, s, 512), BF16),
        jax.ShapeDtypeStruct((bsz, RET_HEADS, s, HEAD_DIM), BF16),
        jax.ShapeDtypeStruct((bsz, RET_HEADS, s, HEAD_DIM), BF16),
        jax.ShapeDtypeStruct((bsz, s, 512), BF16),
        jax.ShapeDtypeStruct((bsz, s, 512), BF16),
        jax.ShapeDtypeStruct((bsz, s, 2048), BF16),
    ]
    out_specs = [
        pl.BlockSpec((1, ATTN_HEADS, tm, HEAD_DIM), head4),
        pl.BlockSpec((1, ATTN_KV_HEADS, tm, HEAD_DIM), head4),
        pl.BlockSpec((1, ATTN_KV_HEADS, tm, LANES), head4),
        pl.BlockSpec((1, tm, 512), tok3),
        pl.BlockSpec((1, RET_HEADS, tm, HEAD_DIM), head4),
        pl.BlockSpec((1, RET_HEADS, tm, HEAD_DIM), head4),
        pl.BlockSpec((1, tm, 512), tok3),
        pl.BlockSpec((1, tm, 512), tok3),
        pl.BlockSpec((1, tm, 2048), tok3),
    ]
    return pl.pallas_call(
        functools.partial(_inproj_kernel, tm=tm),
        grid=(bsz, nt),
        in_specs=[pl.BlockSpec((1, tm, d), tok3),
                  pl.BlockSpec((1, 1, d), lambda b, i: (b, 0, 0)),
                  pl.BlockSpec((1, 1, d), lambda b, i: (b, 0, 0)),
                  pl.BlockSpec((d, n), const),
                  pl.BlockSpec((tm, LANES), lambda b, i: (i, 0)),
                  pl.BlockSpec((tm, LANES), lambda b, i: (i, 0)),
                  pl.BlockSpec((512, 512), const),
                  pl.BlockSpec((1, 512), const),
                  pl.BlockSpec((1, LANES), const)],
        out_specs=out_specs,
        out_shape=out_shape,
        compiler_params=_params("arbitrary", "arbitrary"),
        name="in_proj",
    )(x, mul, shift, w_aug, cos, sin, ones_blk, qg, kg)


def _attn_kernel(q_ref, k_ref, v_ref, z_ref, o_ref, *, tq, tk, seq):
    group = ATTN_HEADS // ATTN_KV_HEADS
    rows = group * tq
    q = q_ref[0].reshape(rows, HEAD_DIM)

    def body(j, carry):
        m, acc = carry
        start = pl.multiple_of(j * tk, tk)
        kc = k_ref[0, 0, pl.ds(start, tk), :]
        vc = v_ref[0, 0, pl.ds(start, tk), :]
        s = _mm_nt(q, kc)
        m_new = jnp.maximum(m, jnp.max(s, axis=1, keepdims=True))
        p = jnp.exp2(s - m_new)
        alpha = jnp.exp2(m - m_new)
        acc = alpha * acc + _mm(p.astype(BF16), vc)
        return m_new, acc

    m0 = jnp.full((rows, 1), -jnp.inf, F32)
    acc0 = jnp.zeros((rows, LANES), F32)
    _, acc = lax.fori_loop(0, seq // tk, body, (m0, acc0))
    out = acc[:, :HEAD_DIM] / acc[:, HEAD_DIM:HEAD_DIM + 1]
    out = jnp.concatenate([out[h * tq:(h + 1) * tq] for h in range(group)], axis=1)
    o_ref[0] = (out * z_ref[0].astype(F32)).astype(BF16)


def _attention(qa, ka, va, za, *, tq, tk):
    bsz, _, s, _ = qa.shape
    group = ATTN_HEADS // ATTN_KV_HEADS
    width = group * HEAD_DIM
    return pl.pallas_call(
        functools.partial(_attn_kernel, tq=tq, tk=tk, seq=s),
        grid=(bsz, ATTN_KV_HEADS, s // tq),
        in_specs=[pl.BlockSpec((1, group, tq, HEAD_DIM), lambda b, g, i: (b, g, i, 0)),
                  pl.BlockSpec((1, 1, s, HEAD_DIM), lambda b, g, i: (b, g, 0, 0)),
                  pl.BlockSpec((1, 1, s, LANES), lambda b, g, i: (b, g, 0, 0)),
                  pl.BlockSpec((1, tq, width), lambda b, g, i: (b, i, g))],
        out_specs=pl.BlockSpec((1, tq, width), lambda b, g, i: (b, i, g)),
        out_shape=jax.ShapeDtypeStruct((bsz, s, ATTN_HEADS * HEAD_DIM), BF16),
        compiler_params=_params("arbitrary", "arbitrary", "arbitrary"),
        name="gqa_attention",
    )(qa, ka, va, za)


def _log_sigmoid(w):
    return -(jnp.maximum(-w, 0.0) + jnp.log1p(jnp.exp(-jnp.abs(w))))


def _ret_kernel(q_ref, k_ref, v_ref, z_ref, wf_ref, wb_ref, gn_ref, o_ref, rb_ref, *, chunk, seq):
    nchunks = seq // chunk
    lg_f = _log_sigmoid(wf_ref[0])[0:1, 0:1]
    lg_b = _log_sigmoid(wb_ref[0])[0:1, 0:1]

    row = lax.broadcasted_iota(jnp.int32, (chunk, chunk), 0).astype(F32)
    colm = lax.broadcasted_iota(jnp.int32, (chunk, chunk), 1).astype(F32)
    diff = row - colm
    decay = jnp.where(diff >= 0, jnp.exp(lg_f * jnp.maximum(diff, 0.0)), jnp.exp(lg_b * jnp.maximum(-diff, 0.0)))
    idx = lax.broadcasted_iota(jnp.int32, (chunk, HEAD_DIM), 0).astype(F32)
    q_dec_f = jnp.exp(lg_f * (idx + 1.0))
    q_dec_b = jnp.exp(lg_b * (chunk - idx))
    k_dec_f = jnp.exp(lg_f * (chunk - 1.0 - idx))
    k_dec_b = jnp.exp(lg_b * idx)
    chunk_dec_f = jnp.exp(lg_f * chunk)
    chunk_dec_b = jnp.exp(lg_b * chunk)

    def load(ref, n):
        start = pl.multiple_of(n * chunk, chunk)
        return ref[0, 0, pl.ds(start, chunk), :]

    def load3(ref, n):
        start = pl.multiple_of(n * chunk, chunk)
        return ref[0, pl.ds(start, chunk), :]

    def bwd(t, state):
        n = nchunks - 1 - t
        rb_ref[n] = state
        kd = (load(k_ref, n).astype(F32) * k_dec_b).astype(BF16)
        return chunk_dec_b * state + _mm_tn(kd, load3(v_ref, n))

    lax.fori_loop(0, nchunks, bwd, jnp.zeros((HEAD_DIM, RET_VALUE_DIM), F32))

    gn = gn_ref[...]

    def fwd(n, state):
        q = load(q_ref, n)
        k = load(k_ref, n)
        v = load3(v_ref, n)
        qf = q.astype(F32)
        scores = (_mm_nt(q, k) * decay).astype(BF16)
        o = _mm(scores, v)
        o += _mm((qf * q_dec_f).astype(BF16), state.astype(BF16))
        o += _mm((qf * q_dec_b).astype(BF16), rb_ref[n].astype(BF16))
        mu = jnp.mean(o, axis=-1, keepdims=True)
        cen = o - mu
        var = jnp.mean(cen * cen, axis=-1, keepdims=True)
        y = cen * lax.rsqrt(var + EPS) * gn * load3(z_ref, n).astype(F32)
        start = pl.multiple_of(n * chunk, chunk)
        o_ref[0, pl.ds(start, chunk), :] = y.astype(BF16)
        kd = (k.astype(F32) * k_dec_f).astype(BF16)
        return chunk_dec_f * state + _mm_tn(kd, v)

    lax.fori_loop(0, nchunks, fwd, jnp.zeros((HEAD_DIM, RET_VALUE_DIM), F32))


def _retention(qr, kr, vr, zr, wf, wb, gn, *, chunk):
    bsz, heads, s, _ = qr.shape
    head_blk = lambda b, h: (b, h, 0, 0)
    col_blk = lambda b, h: (b, 0, h)
    return pl.pallas_call(
        functools.partial(_ret_kernel, chunk=chunk, seq=s),
        grid=(bsz, heads),
        in_specs=[pl.BlockSpec((1, 1, s, HEAD_DIM), head_blk),
                  pl.BlockSpec((1, 1, s, HEAD_DIM), head_blk),
                  pl.BlockSpec((1, s, RET_VALUE_DIM), col_blk),
                  pl.BlockSpec((1, s, RET_VALUE_DIM), col_blk),
                  pl.BlockSpec((1, 8, LANES), lambda b, h: (h, 0, 0)),
                  pl.BlockSpec((1, 8, LANES), lambda b, h: (h, 0, 0)),
                  pl.BlockSpec((1, RET_VALUE_DIM), lambda b, h: (0, h))],
        out_specs=pl.BlockSpec((1, s, RET_VALUE_DIM), col_blk),
        out_shape=jax.ShapeDtypeStruct((bsz, s, heads * RET_VALUE_DIM), BF16),
        scratch_shapes=[pltpu.VMEM((s // chunk, HEAD_DIM, RET_VALUE_DIM), F32)],
        compiler_params=_params("arbitrary", "arbitrary"),
        name="retention",
    )(qr, kr, vr, zr, wf, wb, gn)


def _outproj_kernel(ya_ref, yr_ref, gl_ref, x_ref, gate_ref, wpa_ref, wpr_ref, wout_ref, gpost_ref, o_ref):
    d = x_ref.shape[-1]
    pa = _mm(ya_ref[0], wpa_ref[...])
    pr = _mm(yr_ref[0], wpr_ref[...])
    merged = gl_ref[0, :, :d].astype(F32) * pa + gl_ref[0, :, d:].astype(F32) * pr
    z = _mm(merged.astype(BF16), wout_ref[...])
    ms = jnp.mean(z * z, axis=-1, keepdims=True)
    y = z * lax.rsqrt(ms + EPS) * gpost_ref[...]
    o_ref[0] = x_ref[0] + gate_ref[0] * y


def _outproj(ya, yr, gl, x, gate, wpa, wpr, wout, gpost, *, tm):
    bsz, s, d = x.shape
    tok3 = lambda b, i: (b, i, 0)
    const = lambda b, i: (0, 0)
    return pl.pallas_call(
        _outproj_kernel,
        grid=(bsz, s // tm),
        in_specs=[pl.BlockSpec((1, tm, ya.shape[-1]), tok3),
                  pl.BlockSpec((1, tm, yr.shape[-1]), tok3),
                  pl.BlockSpec((1, tm, 2 * d), tok3),
                  pl.BlockSpec((1, tm, d), tok3),
                  pl.BlockSpec((1, 1, d), lambda b, i: (b, 0, 0)),
                  pl.BlockSpec(wpa.shape, const),
                  pl.BlockSpec(wpr.shape, const),
                  pl.BlockSpec(wout.shape, const),
                  pl.BlockSpec((1, d), const)],
        out_specs=pl.BlockSpec((1, tm, d), tok3),
        out_shape=jax.ShapeDtypeStruct((bsz, s, d), F32),
        compiler_params=_params("arbitrary", "arbitrary"),
        name="out_proj",
    )(ya, yr, gl, x, gate, wpa, wpr, wout, gpost)


def _rope_tables(seq):
    t = jnp.arange(seq)
    row = (t // GRID_W).astype(F32)
    colp = (t % GRID_W).astype(F32)
    half = HEAD_DIM // 2
    inv_freq = ROPE_THETA ** (-jnp.arange(0, half, 2, dtype=F32) / half)
    ang_r = row[:, None] * inv_freq[None, :]
    ang_c = colp[:, None] * inv_freq[None, :]
    cos = jnp.concatenate([jnp.cos(ang_r)] * 2 + [jnp.cos(ang_c)] * 2, axis=-1)
    sin = jnp.concatenate([-jnp.sin(ang_r), jnp.sin(ang_r), -jnp.sin(ang_c), jnp.sin(ang_c)], axis=-1)
    return jnp.tile(cos, (1, 2)), jnp.tile(sin, (1, 2))


def _layer(x, mod, g_pre, w_in, qn_g, kn_g, w_dec_f, w_dec_b, gn_g, w_pa, w_pr, w_out, g_post, tables):
    bsz, s, d = x.shape
    shift, scale, gate = jnp.split(mod, 3, axis=-1)
    mul = (g_pre[None, :] * (1.0 + scale)).reshape(bsz, 1, d)
    shift = shift.reshape(bsz, 1, d)
    gate = gate.reshape(bsz, 1, d)

    qw, kw, vw, zaw, qrw, krw, vrw, zrw, glw = jnp.split(
        w_in, [512, 640, 768, 1280, 1536, 1792, 2304, 2816], axis=1)
    pad = jnp.zeros((d, LANES - HEAD_DIM), w_in.dtype)
    w_aug = jnp.concatenate(
        [qw, kw, vw[:, :HEAD_DIM], pad, vw[:, HEAD_DIM:], pad, zaw, qrw, krw, vrw, zrw, glw], axis=1).astype(BF16)

    blk = jnp.arange(512) // HEAD_DIM
    ones_blk = (blk[:, None] == blk[None, :]).astype(BF16)
    qg = jnp.tile(qn_g * (HEAD_DIM ** -0.5 * LOG2E), ATTN_HEADS).reshape(1, 512)
    kg = jnp.tile(kn_g, ATTN_KV_HEADS).reshape(1, LANES)
    cos, sin = tables

    qa, ka, va, za, qr, kr, vr, zr, gl = _inproj(x, mul, shift, w_aug, cos, sin, ones_blk, qg, kg, tm=512)
    ya = _attention(qa, ka, va, za, tq=128, tk=512)
    wf = jnp.broadcast_to(w_dec_f.astype(F32)[:, None, None], (RET_HEADS, 8, LANES))
    wb = jnp.broadcast_to(w_dec_b.astype(F32)[:, None, None], (RET_HEADS, 8, LANES))
    yr = _retention(qr, kr, vr, zr, wf, wb, gn_g.reshape(1, -1), chunk=128)
    return _outproj(ya, yr, gl, x, gate, w_pa.astype(BF16), w_pr.astype(BF16), w_out.astype(BF16),
                    g_post.reshape(1, d), tm=512)


def kernel(x, c, w_ada, b_ada, g_pre, w_in, qn_g, kn_g, w_dec_f, w_dec_b, gn_g, w_pa, w_pr, w_out, g_post):
    tables = _rope_tables(x.shape[1])
    for l in range(w_ada.shape[0]):
        mod = _ada(c, w_ada[l], b_ada[l])
        x = _layer(x, mod, g_pre[l], w_in[l], qn_g[l], kn_g[l], w_dec_f[l], w_dec_b[l], gn_g[l],
                   w_pa[l], w_pr[l], w_out[l], g_post[l], tables)
    return x
```

```python
import functools
import math

import jax
import jax.numpy as jnp
from jax import lax
from jax.experimental import pallas as pl
from jax.experimental.pallas import tpu as pltpu

GRID_W = 64
ATTN_HEADS = 8
ATTN_KV_HEADS = 2
HEAD_DIM = 64
RET_HEADS = 4
RET_VALUE_DIM = 128
ROPE_THETA = 10000.0
EPS = 1e-6

LANES = 128
VMEM_LIMIT_BYTES = 56 * 1024 * 1024

F32 = jnp.float32
BF16 = jnp.bfloat16
LOG2E = math.log2(math.e)


def _mm(a, b):
    return jnp.dot(a, b, preferred_element_type=F32)


def _mm_nt(a, b):
    return lax.dot_general(a, b, (((1,), (1,)), ((), ())), preferred_element_type=F32)


def _mm_tn(a, b):
    return lax.dot_general(a, b, (((0,), (0,)), ((), ())), preferred_element_type=F32)


def _sigmoid(x):
    return 1.0 / (1.0 + jnp.exp(-x))


def _params(*semantics):
    return pltpu.CompilerParams(dimension_semantics=semantics, vmem_limit_bytes=VMEM_LIMIT_BYTES)


def _ada_kernel(c_ref, w_ref, b_ref, o_ref):
    c = c_ref[...]
    ca = (c * _sigmoid(c)).astype(BF16)
    o_ref[...] = _mm(ca, w_ref[...].astype(BF16)) + b_ref[...]


def _ada(c, w, b):
    bsz, d = c.shape
    n = w.shape[1]
    return pl.pallas_call(
        _ada_kernel,
        grid=(n // d,),
        in_specs=[pl.BlockSpec((bsz, d), lambda j: (0, 0)),
                  pl.BlockSpec((d, d), lambda j: (0, j)),
                  pl.BlockSpec((1, d), lambda j: (0, j))],
        out_specs=pl.BlockSpec((bsz, d), lambda j: (0, j)),
        out_shape=jax.ShapeDtypeStruct((bsz, n), F32),
        compiler_params=_params("arbitrary"),
        name="ada_mod",
    )(c, w, b.reshape(1, n))


def _rope(x, cos, sin, lo):
    partner = jnp.where(lo, pltpu.roll(x, LANES - 16, 1), pltpu.roll(x, 16, 1))
    return x * cos + partner * sin


def _head_rms(x, ones_blk):
    sq = x * x
    hi = sq.astype(BF16)
    lo = (sq - hi.astype(F32)).astype(BF16)
    ms = (_mm(hi, ones_blk) + _mm(lo, ones_blk)) * (1.0 / HEAD_DIM)
    return x * lax.rsqrt(ms + EPS)


def _inproj_kernel(x_ref, mul_ref, shift_ref, w_ref, cos_ref, sin_ref, ones_ref, qg_ref, kg_ref,
                   qa_ref, ka_ref, va_ref, za_ref, qr_ref, kr_ref, vr_ref, zr_ref, gl_ref, *, tm):
    x = x_ref[0]
    ms = jnp.mean(x * x, axis=-1, keepdims=True)
    h = (x * lax.rsqrt(ms + EPS) * mul_ref[0] + shift_ref[0]).astype(BF16)

    cos = cos_ref[...]
    sin = sin_ref[...]
    lane = lax.broadcasted_iota(jnp.int32, (tm, LANES), 1)
    lo = (lane & 31) < 16
    ones_blk = ones_ref[...]

    def rope_cols(y, scale_row):
        outs = []
        for j in range(y.shape[1] // LANES):
            blk = y[:, j * LANES:(j + 1) * LANES]
            if scale_row is not None:
                blk = blk * scale_row[:, j * LANES:(j + 1) * LANES]
            outs.append(_rope(blk, cos, sin, lo))
        return outs

    col = 0

    def seg(width):
        nonlocal col
        acc = _mm(h, w_ref[:, col:col + width])
        col += width
        return acc

    q = _head_rms(seg(512), ones_blk)
    for j, blk in enumerate(rope_cols(q, qg_ref[...])):
        blk = blk.astype(BF16)
        qa_ref[0, 2 * j] = blk[:, :HEAD_DIM]
        qa_ref[0, 2 * j + 1] = blk[:, HEAD_DIM:]
    k = _head_rms(seg(128), ones_blk[:LANES, :LANES])
    blk = rope_cols(k, kg_ref[...])[0].astype(BF16)
    ka_ref[0, 0] = blk[:, :HEAD_DIM]
    ka_ref[0, 1] = blk[:, HEAD_DIM:]
    v = seg(256)
    for g in range(ATTN_KV_HEADS):
        blk = v[:, g * LANES:(g + 1) * LANES]
        va_ref[0, g] = jnp.where(lane == HEAD_DIM, 1.0, blk).astype(BF16)
    z = seg(512)
    za_ref[0] = (z * _sigmoid(z)).astype(BF16)
    for ref, scale in ((qr_ref, None), (kr_ref, HEAD_DIM ** -0.5)):
        for j, blk in enumerate(rope_cols(seg(256), None)):
            if scale is not None:
                blk = blk * scale
            blk = blk.astype(BF16)
            ref[0, 2 * j] = blk[:, :HEAD_DIM]
            ref[0, 2 * j + 1] = blk[:, HEAD_DIM:]
    vr_ref[0] = seg(512).astype(BF16)
    z = seg(512)
    zr_ref[0] = (z * _sigmoid(z)).astype(BF16)
    for j in range(4):
        gl_ref[0, :, j * 512:(j + 1) * 512] = _sigmoid(seg(512)).astype(BF16)


def _inproj(x, mul, shift, w_aug, cos, sin, ones_blk, qg, kg, *, tm):
    bsz, s, d = x.shape
    n = w_aug.shape[1]
    nt = s // tm
    const = lambda b, i: (0, 0)
    tok3 = lambda b, i: (b, i, 0)
    head4 = lambda b, i: (b, 0, i, 0)
    out_shape = [
        jax.ShapeDtypeStruct((bsz, ATTN_HEADS, s, HEAD_DIM), BF16),
        jax.ShapeDtypeStruct((bsz, ATTN_KV_HEADS, s, HEAD_DIM), BF16),
        jax.ShapeDtypeStruct((bsz, ATTN_KV_HEADS, s, LANES), BF16),
        jax.ShapeDtypeStruct((bsz, s, 512), BF16),
        jax.ShapeDtypeStruct((bsz, RET_HEADS, s, HEAD_DIM), BF16),
        jax.ShapeDtypeStruct((bsz, RET_HEADS, s, HEAD_DIM), BF16),
        jax.ShapeDtypeStruct((bsz, s, 512), BF16),
        jax.ShapeDtypeStruct((bsz, s, 512), BF16),
        jax.ShapeDtypeStruct((bsz, s, 2048), BF16),
    ]
    out_specs = [
        pl.BlockSpec((1, ATTN_HEADS, tm, HEAD_DIM), head4),
        pl.BlockSpec((1, ATTN_KV_HEADS, tm, HEAD_DIM), head4),
        pl.BlockSpec((1, ATTN_KV_HEADS, tm, LANES), head4),
        pl.BlockSpec((1, tm, 512), tok3),
        pl.BlockSpec((1, RET_HEADS, tm, HEAD_DIM), head4),
        pl.BlockSpec((1, RET_HEADS, tm, HEAD_DIM), head4),
        pl.BlockSpec((1, tm, 512), tok3),
        pl.BlockSpec((1, tm, 512), tok3),
        pl.BlockSpec((1, tm, 2048), tok3),
    ]
    return pl.pallas_call(
        functools.partial(_inproj_kernel, tm=tm),
        grid=(bsz, nt),
        in_specs=[pl.BlockSpec((1, tm, d), tok3),
                  pl.BlockSpec((1, 1, d), lambda b, i: (b, 0, 0)),
                  pl.BlockSpec((1, 1, d), lambda b, i: (b, 0, 0)),
                  pl.BlockSpec((d, n), const),
                  pl.BlockSpec((tm, LANES), lambda b, i: (i, 0)),
                  pl.BlockSpec((tm, LANES), lambda b, i: (i, 0)),
                  pl.BlockSpec((512, 512), const),
                  pl.BlockSpec((1, 512), const),
                  pl.BlockSpec((1, LANES), const)],
        out_specs=out_specs,
        out_shape=out_shape,
        compiler_params=_params("arbitrary", "arbitrary"),
        name="in_proj",
    )(x, mul, shift, w_aug, cos, sin, ones_blk, qg, kg)


def _attn_kernel(q_ref, k_ref, v_ref, z_ref, o_ref, *, tq, tk, seq):
    group = ATTN_HEADS // ATTN_KV_HEADS
    rows = group * tq
    q = q_ref[0].reshape(rows, HEAD_DIM)

    def body(j, carry):
        m, acc = carry
        start = pl.multiple_of(j * tk, tk)
        kc = k_ref[0, 0, pl.ds(start, tk), :]
        vc = v_ref[0, 0, pl.ds(start, tk), :]
        s = _mm_nt(q, kc)
        m_new = jnp.maximum(m, jnp.max(s, axis=1, keepdims=True))
        p = jnp.exp2(s - m_new)
        alpha = jnp.exp2(m - m_new)
        acc = alpha * acc + _mm(p.astype(BF16), vc)
        return m_new, acc

    m0 = jnp.full((rows, 1), -jnp.inf, F32)
    acc0 = jnp.zeros((rows, LANES), F32)
    _, acc = lax.fori_loop(0, seq // tk, body, (m0, acc0), unroll=True)
    out = acc[:, :HEAD_DIM] / acc[:, HEAD_DIM:HEAD_DIM + 1]
    out = jnp.concatenate([out[h * tq:(h + 1) * tq] for h in range(group)], axis=1)
    o_ref[0] = (out * z_ref[0].astype(F32)).astype(BF16)


def _attention(qa, ka, va, za, *, tq, tk):
    bsz, _, s, _ = qa.shape
    group = ATTN_HEADS // ATTN_KV_HEADS
    width = group * HEAD_DIM
    return pl.pallas_call(
        functools.partial(_attn_kernel, tq=tq, tk=tk, seq=s),
        grid=(bsz, ATTN_KV_HEADS, s // tq),
        in_specs=[pl.BlockSpec((1, group, tq, HEAD_DIM), lambda b, g, i: (b, g, i, 0)),
                  pl.BlockSpec((1, 1, s, HEAD_DIM), lambda b, g, i: (b, g, 0, 0)),
                  pl.BlockSpec((1, 1, s, LANES), lambda b, g, i: (b, g, 0, 0)),
                  pl.BlockSpec((1, tq, width), lambda b, g, i: (b, i, g))],
        out_specs=pl.BlockSpec((1, tq, width), lambda b, g, i: (b, i, g)),
        out_shape=jax.ShapeDtypeStruct((bsz, s, ATTN_HEADS * HEAD_DIM), BF16),
        compiler_params=_params("arbitrary", "arbitrary", "arbitrary"),
        name="gqa_attention",
    )(qa, ka, va, za)


def _log_sigmoid(w):
    return -(jnp.maximum(-w, 0.0) + jnp.log1p(jnp.exp(-jnp.abs(w))))


def _ret_kernel(q_ref, k_ref, v_ref, z_ref, wf_ref, wb_ref, gn_ref, o_ref, rb_ref, *, chunk, seq):
    nchunks = seq // chunk
    lg_f = _log_sigmoid(wf_ref[0])[0:1, 0:1]
    lg_b = _log_sigmoid(wb_ref[0])[0:1, 0:1]

    row = lax.broadcasted_iota(jnp.int32, (chunk, chunk), 0).astype(F32)
    colm = lax.broadcasted_iota(jnp.int32, (chunk, chunk), 1).astype(F32)
    diff = row - colm
    decay = jnp.where(diff >= 0, jnp.exp(lg_f * jnp.maximum(diff, 0.0)), jnp.exp(lg_b * jnp.maximum(-diff, 0.0)))
    idx = lax.broadcasted_iota(jnp.int32, (chunk, HEAD_DIM), 0).astype(F32)
    q_dec_f = jnp.exp(lg_f * (idx + 1.0))
    q_dec_b = jnp.exp(lg_b * (chunk - idx))
    k_dec_f = jnp.exp(lg_f * (chunk - 1.0 - idx))
    k_dec_b = jnp.exp(lg_b * idx)
    chunk_dec_f = jnp.exp(lg_f * chunk)
    chunk_dec_b = jnp.exp(lg_b * chunk)

    def load(ref, n):
        start = pl.multiple_of(n * chunk, chunk)
        return ref[0, 0, pl.ds(start, chunk), :]

    def load3(ref, n):
        start = pl.multiple_of(n * chunk, chunk)
        return ref[0, pl.ds(start, chunk), :]

    def bwd(t, state):
        n = nchunks - 1 - t
        rb_ref[n] = state
        kd = (load(k_ref, n).astype(F32) * k_dec_b).astype(BF16)
        return chunk_dec_b * state + _mm_tn(kd, load3(v_ref, n))

    lax.fori_loop(0, nchunks, bwd, jnp.zeros((HEAD_DIM, RET_VALUE_DIM), F32))

    gn = gn_ref[...]

    def fwd(n, state):
        q = load(q_ref, n)
        k = load(k_ref, n)
        v = load3(v_ref, n)
        qf = q.astype(F32)
        scores = (_mm_nt(q, k) * decay).astype(BF16)
        o = _mm(scores, v)
        o += _mm((qf * q_dec_f).astype(BF16), state.astype(BF16))
        o += _mm((qf * q_dec_b).astype(BF16), rb_ref[n].astype(BF16))
        mu = jnp.mean(o, axis=-1, keepdims=True)
        cen = o - mu
        var = jnp.mean(cen * cen, axis=-1, keepdims=True)
        y = cen * lax.rsqrt(var + EPS) * gn * load3(z_ref, n).astype(F32)
        start = pl.multiple_of(n * chunk, chunk)
        o_ref[0, pl.ds(start, chunk), :] = y.astype(BF16)
        kd = (k.astype(F32) * k_dec_f).astype(BF16)
        return chunk_dec_f * state + _mm_tn(kd, v)

    lax.fori_loop(0, nchunks, fwd, jnp.zeros((HEAD_DIM, RET_VALUE_DIM), F32))


def _retention(qr, kr, vr, zr, wf, wb, gn, *, chunk):
    bsz, heads, s, _ = qr.shape
    head_blk = lambda b, h: (b, h, 0, 0)
    col_blk = lambda b, h: (b, 0, h)
    return pl.pallas_call(
        functools.partial(_ret_kernel, chunk=chunk, seq=s),
        grid=(bsz, heads),
        in_specs=[pl.BlockSpec((1, 1, s, HEAD_DIM), head_blk),
                  pl.BlockSpec((1, 1, s, HEAD_DIM), head_blk),
                  pl.BlockSpec((1, s, RET_VALUE_DIM), col_blk),
                  pl.BlockSpec((1, s, RET_VALUE_DIM), col_blk),
                  pl.BlockSpec((1, 8, LANES), lambda b, h: (h, 0, 0)),
                  pl.BlockSpec((1, 8, LANES), lambda b, h: (h, 0, 0)),
                  pl.BlockSpec((1, RET_VALUE_DIM), lambda b, h: (0, h))],
        out_specs=pl.BlockSpec((1, s, RET_VALUE_DIM), col_blk),
        out_shape=jax.ShapeDtypeStruct((bsz, s, heads * RET_VALUE_DIM), BF16),
        scratch_shapes=[pltpu.VMEM((s // chunk, HEAD_DIM, RET_VALUE_DIM), F32)],
        compiler_params=_params("arbitrary", "arbitrary"),
        name="retention",
    )(qr, kr, vr, zr, wf, wb, gn)


def _outproj_kernel(ya_ref, yr_ref, gl_ref, x_ref, gate_ref, wpa_ref, wpr_ref, wout_ref, gpost_ref, o_ref):
    d = x_ref.shape[-1]
    pa = _mm(ya_ref[0], wpa_ref[...])
    pr = _mm(yr_ref[0], wpr_ref[...])
    merged = gl_ref[0, :, :d].astype(F32) * pa + gl_ref[0, :, d:].astype(F32) * pr
    z = _mm(merged.astype(BF16), wout_ref[...])
    ms = jnp.mean(z * z, axis=-1, keepdims=True)
    y = z * lax.rsqrt(ms + EPS) * gpost_ref[...]
    o_ref[0] = x_ref[0] + gate_ref[0] * y


def _outproj(ya, yr, gl, x, gate, wpa, wpr, wout, gpost, *, tm):
    bsz, s, d = x.shape
    tok3 = lambda b, i: (b, i, 0)
    const = lambda b, i: (0, 0)
    return pl.pallas_call(
        _outproj_kernel,
        grid=(bsz, s // tm),
        in_specs=[pl.BlockSpec((1, tm, ya.shape[-1]), tok3),
                  pl.BlockSpec((1, tm, yr.shape[-1]), tok3),
                  pl.BlockSpec((1, tm, 2 * d), tok3),
                  pl.BlockSpec((1, tm, d), tok3),
                  pl.BlockSpec((1, 1, d), lambda b, i: (b, 0, 0)),
                  pl.BlockSpec(wpa.shape, const),
                  pl.BlockSpec(wpr.shape, const),
                  pl.BlockSpec(wout.shape, const),
                  pl.BlockSpec((1, d), const)],
        out_specs=pl.BlockSpec((1, tm, d), tok3),
        out_shape=jax.ShapeDtypeStruct((bsz, s, d), F32),
        compiler_params=_params("arbitrary", "arbitrary"),
        name="out_proj",
    )(ya, yr, gl, x, gate, wpa, wpr, wout, gpost)


def _rope_tables(seq):
    t = jnp.arange(seq)
    row = (t // GRID_W).astype(F32)
    colp = (t % GRID_W).astype(F32)
    half = HEAD_DIM // 2
    inv_freq = ROPE_THETA ** (-jnp.arange(0, half, 2, dtype=F32) / half)
    ang_r = row[:, None] * inv_freq[None, :]
    ang_c = colp[:, None] * inv_freq[None, :]
    cos = jnp.concatenate([jnp.cos(ang_r)] * 2 + [jnp.cos(ang_c)] * 2, axis=-1)
    sin = jnp.concatenate([-jnp.sin(ang_r), jnp.sin(ang_r), -jnp.sin(ang_c), jnp.sin(ang_c)], axis=-1)
    return jnp.tile(cos, (1, 2)), jnp.tile(sin, (1, 2))


def _layer(x, mod, g_pre, w_in, qn_g, kn_g, w_dec_f, w_dec_b, gn_g, w_pa, w_pr, w_out, g_post, tables):
    bsz, s, d = x.shape
    shift, scale, gate = jnp.split(mod, 3, axis=-1)
    mul = (g_pre[None, :] * (1.0 + scale)).reshape(bsz, 1, d)
    shift = shift.reshape(bsz, 1, d)
    gate = gate.reshape(bsz, 1, d)

    qw, kw, vw, zaw, qrw, krw, vrw, zrw, glw = jnp.split(
        w_in, [512, 640, 768, 1280, 1536, 1792, 2304, 2816], axis=1)
    pad = jnp.zeros((d, LANES - HEAD_DIM), w_in.dtype)
    w_aug = jnp.concatenate(
        [qw, kw, vw[:, :HEAD_DIM], pad, vw[:, HEAD_DIM:], pad, zaw, qrw, krw, vrw, zrw, glw], axis=1).astype(BF16)

    blk = jnp.arange(512) // HEAD_DIM
    ones_blk = (blk[:, None] == blk[None, :]).astype(BF16)
    qg = jnp.tile(qn_g * (HEAD_DIM ** -0.5 * LOG2E), ATTN_HEADS).reshape(1, 512)
    kg = jnp.tile(kn_g, ATTN_KV_HEADS).reshape(1, LANES)
    cos, sin = tables

    qa, ka, va, za, qr, kr, vr, zr, gl = _inproj(x, mul, shift, w_aug, cos, sin, ones_blk, qg, kg, tm=512)
    ya = _attention(qa, ka, va, za, tq=128, tk=512)
    wf = jnp.broadcast_to(w_dec_f.astype(F32)[:, None, None], (RET_HEADS, 8, LANES))
    wb = jnp.broadcast_to(w_dec_b.astype(F32)[:, None, None], (RET_HEADS, 8, LANES))
    yr = _retention(qr, kr, vr, zr, wf, wb, gn_g.reshape(1, -1), chunk=128)
    return _outproj(ya, yr, gl, x, gate, w_pa.astype(BF16), w_pr.astype(BF16), w_out.astype(BF16),
                    g_post.reshape(1, d), tm=512)


def kernel(x, c, w_ada, b_ada, g_pre, w_in, qn_g, kn_g, w_dec_f, w_dec_b, gn_g, w_pa, w_pr, w_out, g_post):
    tables = _rope_tables(x.shape[1])
    for l in range(w_ada.shape[0]):
        mod = _ada(c, w_ada[l], b_ada[l])
        x = _layer(x, mod, g_pre[l], w_in[l], qn_g[l], kn_g[l], w_dec_f[l], w_dec_b[l], gn_g[l],
                   w_pa[l], w_pr[l], w_out[l], g_post[l], tables)
    return x
```

```python
import functools
import math

import jax
import jax.numpy as jnp
from jax import lax
from jax.experimental import pallas as pl
from jax.experimental.pallas import tpu as pltpu

GRID_W = 64
ATTN_HEADS = 8
ATTN_KV_HEADS = 2
HEAD_DIM = 64
RET_HEADS = 4
RET_VALUE_DIM = 128
ROPE_THETA = 10000.0
EPS = 1e-6

LANES = 128
VMEM_LIMIT_BYTES = 56 * 1024 * 1024

F32 = jnp.float32
BF16 = jnp.bfloat16
LOG2E = math.log2(math.e)


def _mm(a, b):
    return jnp.dot(a, b, preferred_element_type=F32)


def _mm_nt(a, b):
    return lax.dot_general(a, b, (((1,), (1,)), ((), ())), preferred_element_type=F32)


def _mm_tn(a, b):
    return lax.dot_general(a, b, (((0,), (0,)), ((), ())), preferred_element_type=F32)


def _sigmoid(x):
    return 1.0 / (1.0 + jnp.exp(-x))


def _params(*semantics):
    return pltpu.CompilerParams(dimension_semantics=semantics, vmem_limit_bytes=VMEM_LIMIT_BYTES)


def _ada_kernel(c_ref, w_ref, b_ref, o_ref):
    c = c_ref[...]
    ca = (c * _sigmoid(c)).astype(BF16)
    o_ref[...] = _mm(ca, w_ref[...].astype(BF16)) + b_ref[...]


def _ada(c, w, b):
    bsz, d = c.shape
    n = w.shape[1]
    return pl.pallas_call(
        _ada_kernel,
        grid=(n // d,),
        in_specs=[pl.BlockSpec((bsz, d), lambda j: (0, 0)),
                  pl.BlockSpec((d, d), lambda j: (0, j)),
                  pl.BlockSpec((1, d), lambda j: (0, j))],
        out_specs=pl.BlockSpec((bsz, d), lambda j: (0, j)),
        out_shape=jax.ShapeDtypeStruct((bsz, n), F32),
        compiler_params=_params("arbitrary"),
        name="ada_mod",
    )(c, w, b.reshape(1, n))


def _rope(x, cos, sin, lo):
    partner = jnp.where(lo, pltpu.roll(x, LANES - 16, 1), pltpu.roll(x, 16, 1))
    return x * cos + partner * sin


def _head_rms(x, ones_blk):
    ms = _mm((x * x).astype(BF16), ones_blk) * (1.0 / HEAD_DIM)
    return x * lax.rsqrt(ms + EPS)


def _inproj_kernel(x_ref, mul_ref, shift_ref, w_ref, cos_ref, sin_ref, ones_ref, qg_ref, kg_ref,
                   qa_ref, ka_ref, va_ref, za_ref, qr_ref, kr_ref, vr_ref, zr_ref, gl_ref, *, tm):
    x = x_ref[0]
    ms = jnp.mean(x * x, axis=-1, keepdims=True)
    h = (x * lax.rsqrt(ms + EPS) * mul_ref[0] + shift_ref[0]).astype(BF16)

    cos = cos_ref[...]
    sin = sin_ref[...]
    lane = lax.broadcasted_iota(jnp.int32, (tm, LANES), 1)
    lo = (lane & 31) < 16
    ones_blk = ones_ref[...]

    def rope_cols(y, scale_row):
        outs = []
        for j in range(y.shape[1] // LANES):
            blk = y[:, j * LANES:(j + 1) * LANES]
            if scale_row is not None:
                blk = blk * scale_row[:, j * LANES:(j + 1) * LANES]
            outs.append(_rope(blk, cos, sin, lo))
        return outs

    col = 0

    def seg(width):
        nonlocal col
        acc = _mm(h, w_ref[:, col:col + width])
        col += width
        return acc

    q = _head_rms(seg(512), ones_blk)
    for j, blk in enumerate(rope_cols(q, qg_ref[...])):
        blk = blk.astype(BF16)
        qa_ref[0, 2 * j] = blk[:, :HEAD_DIM]
        qa_ref[0, 2 * j + 1] = blk[:, HEAD_DIM:]
    k = _head_rms(seg(128), ones_blk[:LANES, :LANES])
    blk = rope_cols(k, kg_ref[...])[0].astype(BF16)
    ka_ref[0, 0] = blk[:, :HEAD_DIM]
    ka_ref[0, 1] = blk[:, HEAD_DIM:]
    v = seg(256)
    for g in range(ATTN_KV_HEADS):
        blk = v[:, g * LANES:(g + 1) * LANES]
        va_ref[0, g] = jnp.where(lane == HEAD_DIM, 1.0, blk).astype(BF16)
    z = seg(512)
    za_ref[0] = (z * _sigmoid(z)).astype(BF16)
    for ref, scale in ((qr_ref, None), (kr_ref, HEAD_DIM ** -0.5)):
        for j, blk in enumerate(rope_cols(seg(256), None)):
            if scale is not None:
                blk = blk * scale
            blk = blk.astype(BF16)
            ref[0, 2 * j] = blk[:, :HEAD_DIM]
            ref[0, 2 * j + 1] = blk[:, HEAD_DIM:]
    vr_ref[0] = seg(512).astype(BF16)
    z = seg(512)
    zr_ref[0] = (z * _sigmoid(z)).astype(BF16)
    for j in range(4):
        gl_ref[0, :, j * 512:(j + 1) * 512] = _sigmoid(seg(512)).astype(BF16)


def _inproj(x, mul, shift, w_aug, cos, sin, ones_blk, qg, kg, *, tm):
    bsz, s, d = x.shape
    n = w_aug.shape[1]
    nt = s // tm
    const = lambda b, i: (0, 0)
    tok3 = lambda b, i: (b, i, 0)
    head4 = lambda b, i: (b, 0, i, 0)
    out_shape = [
        jax.ShapeDtypeStruct((bsz, ATTN_HEADS, s, HEAD_DIM), BF16),
        jax.ShapeDtypeStruct((bsz, ATTN_KV_HEADS, s, HEAD_DIM), BF16),
        jax.ShapeDtypeStruct((bsz, ATTN_KV_HEADS, s, LANES), BF16),
        jax.ShapeDtypeStruct((bsz, s, 512), BF16),
        jax.ShapeDtypeStruct((bsz, RET_HEADS, s, HEAD_DIM), BF16),
        jax.ShapeDtypeStruct((bsz, RET_HEADS, s, HEAD_DIM), BF16),
        jax.ShapeDtypeStruct((bsz, s, 512), BF16),
        jax.ShapeDtypeStruct((bsz, s, 512), BF16),
        jax.ShapeDtypeStruct((bsz, s, 2048), BF16),
    ]
    out_specs = [
        pl.BlockSpec((1, ATTN_HEADS, tm, HEAD_DIM), head4),
        pl.BlockSpec((1, ATTN_KV_HEADS, tm, HEAD_DIM), head4),
        pl.BlockSpec((1, ATTN_KV_HEADS, tm, LANES), head4),
        pl.BlockSpec((1, tm, 512), tok3),
        pl.BlockSpec((1, RET_HEADS, tm, HEAD_DIM), head4),
        pl.BlockSpec((1, RET_HEADS, tm, HEAD_DIM), head4),
        pl.BlockSpec((1, tm, 512), tok3),
        pl.BlockSpec((1, tm, 512), tok3),
        pl.BlockSpec((1, tm, 2048), tok3),
    ]
    return pl.pallas_call(
        functools.partial(_inproj_kernel, tm=tm),
        grid=(bsz, nt),
        in_specs=[pl.BlockSpec((1, tm, d), tok3),
                  pl.BlockSpec((1, 1, d), lambda b, i: (b, 0, 0)),
                  pl.BlockSpec((1, 1, d), lambda b, i: (b, 0, 0)),
                  pl.BlockSpec((d, n), const),
                  pl.BlockSpec((tm, LANES), lambda b, i: (i, 0)),
                  pl.BlockSpec((tm, LANES), lambda b, i: (i, 0)),
                  pl.BlockSpec((512, 512), const),
                  pl.BlockSpec((1, 512), const),
                  pl.BlockSpec((1, LANES), const)],
        out_specs=out_specs,
        out_shape=out_shape,
        compiler_params=_params("arbitrary", "arbitrary"),
        name="in_proj",
    )(x, mul, shift, w_aug, cos, sin, ones_blk, qg, kg)


def _attn_kernel(q_ref, k_ref, v_ref, z_ref, o_ref, *, tq, tk, seq):
    group = ATTN_HEADS // ATTN_KV_HEADS
    rows = group * tq
    q = q_ref[0].reshape(rows, HEAD_DIM)

    def body(j, carry):
        m, acc = carry
        start = pl.multiple_of(j * tk, tk)
        kc = k_ref[0, 0, pl.ds(start, tk), :]
        vc = v_ref[0, 0, pl.ds(start, tk), :]
        s = _mm_nt(q, kc)
        m_new = jnp.maximum(m, jnp.max(s, axis=1, keepdims=True))
        p = jnp.exp2(s - m_new)
        alpha = jnp.exp2(m - m_new)
        acc = alpha * acc + _mm(p.astype(BF16), vc)
        return m_new, acc

    m0 = jnp.full((rows, 1), -jnp.inf, F32)
    acc0 = jnp.zeros((rows, LANES), F32)
    _, acc = lax.fori_loop(0, seq // tk, body, (m0, acc0), unroll=True)
    out = acc[:, :HEAD_DIM] / acc[:, HEAD_DIM:HEAD_DIM + 1]
    out = jnp.concatenate([out[h * tq:(h + 1) * tq] for h in range(group)], axis=1)
    o_ref[0] = (out * z_ref[0].astype(F32)).astype(BF16)


def _attention(qa, ka, va, za, *, tq, tk):
    bsz, _, s, _ = qa.shape
    group = ATTN_HEADS // ATTN_KV_HEADS
    width = group * HEAD_DIM
    return pl.pallas_call(
        functools.partial(_attn_kernel, tq=tq, tk=tk, seq=s),
        grid=(bsz, ATTN_KV_HEADS, s // tq),
        in_specs=[pl.BlockSpec((1, group, tq, HEAD_DIM), lambda b, g, i: (b, g, i, 0)),
                  pl.BlockSpec((1, 1, s, HEAD_DIM), lambda b, g, i: (b, g, 0, 0)),
                  pl.BlockSpec((1, 1, s, LANES), lambda b, g, i: (b, g, 0, 0)),
                  pl.BlockSpec((1, tq, width), lambda b, g, i: (b, i, g))],
        out_specs=pl.BlockSpec((1, tq, width), lambda b, g, i: (b, i, g)),
        out_shape=jax.ShapeDtypeStruct((bsz, s, ATTN_HEADS * HEAD_DIM), BF16),
        compiler_params=_params("arbitrary", "arbitrary", "arbitrary"),
        name="gqa_attention",
    )(qa, ka, va, za)


def _log_sigmoid(w):
    return -(jnp.maximum(-w, 0.0) + jnp.log1p(jnp.exp(-jnp.abs(w))))


def _ret_kernel(q_ref, k_ref, v_ref, z_ref, wf_ref, wb_ref, gn_ref, o_ref, rb_ref, *, chunk, seq, unroll):
    nchunks = seq // chunk
    lg_f = _log_sigmoid(wf_ref[0])[0:1, 0:1]
    lg_b = _log_sigmoid(wb_ref[0])[0:1, 0:1]

    row = lax.broadcasted_iota(jnp.int32, (chunk, chunk), 0).astype(F32)
    colm = lax.broadcasted_iota(jnp.int32, (chunk, chunk), 1).astype(F32)
    diff = row - colm
    decay = jnp.where(diff >= 0, jnp.exp(lg_f * jnp.maximum(diff, 0.0)), jnp.exp(lg_b * jnp.maximum(-diff, 0.0)))
    idx = lax.broadcasted_iota(jnp.int32, (chunk, HEAD_DIM), 0).astype(F32)
    q_dec_f = jnp.exp(lg_f * (idx + 1.0))
    q_dec_b = jnp.exp(lg_b * (chunk - idx))
    k_dec_f = jnp.exp(lg_f * (chunk - 1.0 - idx))
    k_dec_b = jnp.exp(lg_b * idx)
    chunk_dec_f = jnp.exp(lg_f * chunk)
    chunk_dec_b = jnp.exp(lg_b * chunk)

    def load(ref, n):
        start = pl.multiple_of(n * chunk, chunk)
        return ref[0, 0, pl.ds(start, chunk), :]

    def load3(ref, n):
        start = pl.multiple_of(n * chunk, chunk)
        return ref[0, pl.ds(start, chunk), :]

    def bwd(t, state):
        n = nchunks - 1 - t
        rb_ref[n] = state
        kd = (load(k_ref, n).astype(F32) * k_dec_b).astype(BF16)
        return chunk_dec_b * state + _mm_tn(kd, load3(v_ref, n))

    lax.fori_loop(0, nchunks, bwd, jnp.zeros((HEAD_DIM, RET_VALUE_DIM), F32), unroll=unroll)

    gn = gn_ref[...]

    def fwd(n, state):
        q = load(q_ref, n)
        k = load(k_ref, n)
        v = load3(v_ref, n)
        qf = q.astype(F32)
        scores = (_mm_nt(q, k) * decay).astype(BF16)
        o = _mm(scores, v)
        o += _mm((qf * q_dec_f).astype(BF16), state.astype(BF16))
        o += _mm((qf * q_dec_b).astype(BF16), rb_ref[n].astype(BF16))
        mu = jnp.mean(o, axis=-1, keepdims=True)
        cen = o - mu
        var = jnp.mean(cen * cen, axis=-1, keepdims=True)
        y = cen * lax.rsqrt(var + EPS) * gn * load3(z_ref, n).astype(F32)
        start = pl.multiple_of(n * chunk, chunk)
        o_ref[0, pl.ds(start, chunk), :] = y.astype(BF16)
        kd = (k.astype(F32) * k_dec_f).astype(BF16)
        return chunk_dec_f * state + _mm_tn(kd, v)

    lax.fori_loop(0, nchunks, fwd, jnp.zeros((HEAD_DIM, RET_VALUE_DIM), F32), unroll=unroll)


def _retention(qr, kr, vr, zr, wf, wb, gn, *, chunk, unroll):
    bsz, heads, s, _ = qr.shape
    head_blk = lambda b, h: (b, h, 0, 0)
    col_blk = lambda b, h: (b, 0, h)
    return pl.pallas_call(
        functools.partial(_ret_kernel, chunk=chunk, seq=s, unroll=unroll),
        grid=(bsz, heads),
        in_specs=[pl.BlockSpec((1, 1, s, HEAD_DIM), head_blk),
                  pl.BlockSpec((1, 1, s, HEAD_DIM), head_blk),
                  pl.BlockSpec((1, s, RET_VALUE_DIM), col_blk),
                  pl.BlockSpec((1, s, RET_VALUE_DIM), col_blk),
                  pl.BlockSpec((1, 8, LANES), lambda b, h: (h, 0, 0)),
                  pl.BlockSpec((1, 8, LANES), lambda b, h: (h, 0, 0)),
                  pl.BlockSpec((1, RET_VALUE_DIM), lambda b, h: (0, h))],
        out_specs=pl.BlockSpec((1, s, RET_VALUE_DIM), col_blk),
        out_shape=jax.ShapeDtypeStruct((bsz, s, heads * RET_VALUE_DIM), BF16),
        scratch_shapes=[pltpu.VMEM((s // chunk, HEAD_DIM, RET_VALUE_DIM), F32)],
        compiler_params=_params("arbitrary", "arbitrary"),
        name="retention",
    )(qr, kr, vr, zr, wf, wb, gn)


def _outproj_kernel(ya_ref, yr_ref, gl_ref, x_ref, gate_ref, wpa_ref, wpr_ref, wout_ref, gpost_ref, o_ref):
    d = x_ref.shape[-1]
    pa = _mm(ya_ref[0], wpa_ref[...])
    pr = _mm(yr_ref[0], wpr_ref[...])
    merged = gl_ref[0, :, :d].astype(F32) * pa + gl_ref[0, :, d:].astype(F32) * pr
    z = _mm(merged.astype(BF16), wout_ref[...])
    ms = jnp.mean(z * z, axis=-1, keepdims=True)
    y = z * lax.rsqrt(ms + EPS) * gpost_ref[...]
    o_ref[0] = x_ref[0] + gate_ref[0] * y


def _outproj(ya, yr, gl, x, gate, wpa, wpr, wout, gpost, *, tm):
    bsz, s, d = x.shape
    tok3 = lambda b, i: (b, i, 0)
    const = lambda b, i: (0, 0)
    return pl.pallas_call(
        _outproj_kernel,
        grid=(bsz, s // tm),
        in_specs=[pl.BlockSpec((1, tm, ya.shape[-1]), tok3),
                  pl.BlockSpec((1, tm, yr.shape[-1]), tok3),
                  pl.BlockSpec((1, tm, 2 * d), tok3),
                  pl.BlockSpec((1, tm, d), tok3),
                  pl.BlockSpec((1, 1, d), lambda b, i: (b, 0, 0)),
                  pl.BlockSpec(wpa.shape, const),
                  pl.BlockSpec(wpr.shape, const),
                  pl.BlockSpec(wout.shape, const),
                  pl.BlockSpec((1, d), const)],
        out_specs=pl.BlockSpec((1, tm, d), tok3),
        out_shape=jax.ShapeDtypeStruct((bsz, s, d), F32),
        compiler_params=_params("arbitrary", "arbitrary"),
        name="out_proj",
    )(ya, yr, gl, x, gate, wpa, wpr, wout, gpost)


def _rope_tables(seq):
    t = jnp.arange(seq)
    row = (t // GRID_W).astype(F32)
    colp = (t % GRID_W).astype(F32)
    half = HEAD_DIM // 2
    inv_freq = ROPE_THETA ** (-jnp.arange(0, half, 2, dtype=F32) / half)
    ang_r = row[:, None] * inv_freq[None, :]
    ang_c = colp[:, None] * inv_freq[None, :]
    cos = jnp.concatenate([jnp.cos(ang_r)] * 2 + [jnp.cos(ang_c)] * 2, axis=-1)
    sin = jnp.concatenate([-jnp.sin(ang_r), jnp.sin(ang_r), -jnp.sin(ang_c), jnp.sin(ang_c)], axis=-1)
    return jnp.tile(cos, (1, 2)), jnp.tile(sin, (1, 2))


def _layer(x, mod, g_pre, w_in, qn_g, kn_g, w_dec_f, w_dec_b, gn_g, w_pa, w_pr, w_out, g_post, tables):
    bsz, s, d = x.shape
    shift, scale, gate = jnp.split(mod, 3, axis=-1)
    mul = (g_pre[None, :] * (1.0 + scale)).reshape(bsz, 1, d)
    shift = shift.reshape(bsz, 1, d)
    gate = gate.reshape(bsz, 1, d)

    qw, kw, vw, zaw, qrw, krw, vrw, zrw, glw = jnp.split(
        w_in, [512, 640, 768, 1280, 1536, 1792, 2304, 2816], axis=1)
    pad = jnp.zeros((d, LANES - HEAD_DIM), w_in.dtype)
    w_aug = jnp.concatenate(
        [qw, kw, vw[:, :HEAD_DIM], pad, vw[:, HEAD_DIM:], pad, zaw, qrw, krw, vrw, zrw, glw], axis=1).astype(BF16)

    blk = jnp.arange(512) // HEAD_DIM
    ones_blk = (blk[:, None] == blk[None, :]).astype(BF16)
    qg = jnp.tile(qn_g * (HEAD_DIM ** -0.5 * LOG2E), ATTN_HEADS).reshape(1, 512)
    kg = jnp.tile(kn_g, ATTN_KV_HEADS).reshape(1, LANES)
    cos, sin = tables

    qa, ka, va, za, qr, kr, vr, zr, gl = _inproj(x, mul, shift, w_aug, cos, sin, ones_blk, qg, kg, tm=512)
    ya = _attention(qa, ka, va, za, tq=128, tk=s)
    wf = jnp.broadcast_to(w_dec_f.astype(F32)[:, None, None], (RET_HEADS, 8, LANES))
    wb = jnp.broadcast_to(w_dec_b.astype(F32)[:, None, None], (RET_HEADS, 8, LANES))
    yr = _retention(qr, kr, vr, zr, wf, wb, gn_g.reshape(1, -1), chunk=256, unroll=16)
    return _outproj(ya, yr, gl, x, gate, w_pa.astype(BF16), w_pr.astype(BF16), w_out.astype(BF16),
                    g_post.reshape(1, d), tm=512)


def kernel(x, c, w_ada, b_ada, g_pre, w_in, qn_g, kn_g, w_dec_f, w_dec_b, gn_g, w_pa, w_pr, w_out, g_post):
    tables = _rope_tables(x.shape[1])
    for l in range(w_ada.shape[0]):
        mod = _ada(c, w_ada[l], b_ada[l])
        x = _layer(x, mod, g_pre[l], w_in[l], qn_g[l], kn_g[l], w_dec_f[l], w_dec_b[l], gn_g[l],
                   w_pa[l], w_pr[l], w_out[l], g_post[l], tables)
    return x
```

```python
import functools
import math

import jax
import jax.numpy as jnp
from jax import lax
from jax.experimental import pallas as pl
from jax.experimental.pallas import tpu as pltpu

GRID_W = 64
ATTN_HEADS = 8
ATTN_KV_HEADS = 2
HEAD_DIM = 64
RET_HEADS = 4
RET_VALUE_DIM = 128
ROPE_THETA = 10000.0
EPS = 1e-6

IN_SIZES = {"qa": 512, "ka": 128, "va": 128, "za": 512, "qr": 256, "kr": 256, "vr": 512, "zr": 512, "gl": 2048}
IN_OFFSETS = dict(zip(IN_SIZES, [sum(list(IN_SIZES.values())[:i]) for i in range(len(IN_SIZES))]))

LANES = 128
VMEM_LIMIT_BYTES = 56 * 1024 * 1024

F32 = jnp.float32
BF16 = jnp.bfloat16
LOG2E = math.log2(math.e)


def _mm(a, b):
    return jnp.dot(a, b, preferred_element_type=F32)


def _mm_nt(a, b):
    return lax.dot_general(a, b, (((1,), (1,)), ((), ())), preferred_element_type=F32)


def _mm_tn(a, b):
    return lax.dot_general(a, b, (((0,), (0,)), ((), ())), preferred_element_type=F32)


def _sigmoid(x):
    return 1.0 / (1.0 + jnp.exp(-x))


def _params(*semantics):
    return pltpu.CompilerParams(dimension_semantics=semantics, vmem_limit_bytes=VMEM_LIMIT_BYTES)


def _ada_kernel(c_ref, w_ref, b_ref, o_ref):
    c = c_ref[...]
    ca = (c * _sigmoid(c)).astype(BF16)
    o_ref[...] = _mm(ca, w_ref[...].astype(BF16)) + b_ref[...]


def _ada(c, w, b):
    bsz, d = c.shape
    n = w.shape[1]
    return pl.pallas_call(
        _ada_kernel,
        grid=(n // d,),
        in_specs=[pl.BlockSpec((bsz, d), lambda j: (0, 0)),
                  pl.BlockSpec((d, d), lambda j: (0, j)),
                  pl.BlockSpec((1, d), lambda j: (0, j))],
        out_specs=pl.BlockSpec((bsz, d), lambda j: (0, j)),
        out_shape=jax.ShapeDtypeStruct((bsz, n), F32),
        compiler_params=_params("arbitrary"),
        name="ada_mod",
    )(c, w, b.reshape(1, n))


def _rope(x, cos, sin, lo):
    partner = jnp.where(lo, pltpu.roll(x, LANES - 16, 1), pltpu.roll(x, 16, 1))
    return x * cos + partner * sin


def _head_rms(x, ones_blk):
    ms = _mm((x * x).astype(BF16), ones_blk) * (1.0 / HEAD_DIM)
    return x * lax.rsqrt(ms + EPS)


def _inproj_kernel(x_ref, mul_ref, shift_ref, w_ref, cos_ref, sin_ref, ones_ref, qg_ref, kg_ref,
                   qa_ref, ka_ref, va_ref, za_ref, qr_ref, kr_ref, vr_ref, zr_ref, gl_ref, *, tm):
    x = x_ref[0]
    ms = jnp.mean(x * x, axis=-1, keepdims=True)
    h = (x * lax.rsqrt(ms + EPS) * mul_ref[0] + shift_ref[0]).astype(BF16)

    cos = cos_ref[...]
    sin = sin_ref[...]
    lane = lax.broadcasted_iota(jnp.int32, (tm, LANES), 1)
    lo = (lane & 31) < 16
    ones_blk = ones_ref[...]

    def rope_cols(y, scale_row):
        outs = []
        for j in range(y.shape[1] // LANES):
            blk = y[:, j * LANES:(j + 1) * LANES]
            if scale_row is not None:
                blk = blk * scale_row[:, j * LANES:(j + 1) * LANES]
            outs.append(_rope(blk, cos, sin, lo))
        return outs

    def seg(name, part=0, width=None):
        start = IN_OFFSETS[name] + part
        width = IN_SIZES[name] if width is None else width
        return _mm(h, w_ref[:, start:start + width])

    def store_heads(ref, blks):
        for j, blk in enumerate(blks):
            blk = blk.astype(BF16)
            ref[0, 2 * j] = blk[:, :HEAD_DIM]
            ref[0, 2 * j + 1] = blk[:, HEAD_DIM:]

    for j in range(IN_SIZES["gl"] // 512):
        gl_ref[0, :, j * 512:(j + 1) * 512] = _sigmoid(seg("gl", j * 512, 512)).astype(BF16)
    z = seg("za")
    za_ref[0] = (z * _sigmoid(z)).astype(BF16)
    z = seg("zr")
    zr_ref[0] = (z * _sigmoid(z)).astype(BF16)
    q = _head_rms(seg("qa"), ones_blk)
    store_heads(qa_ref, rope_cols(q, qg_ref[...]))
    kv = seg("ka", 0, IN_SIZES["ka"] + IN_SIZES["va"])
    k = _head_rms(kv[:, :LANES], ones_blk[:LANES, :LANES])
    store_heads(ka_ref, rope_cols(k, kg_ref[...]))
    v = kv[:, LANES:]
    for g, blk in enumerate((v, pltpu.roll(v, HEAD_DIM, 1))):
        va_ref[0, g] = jnp.where(lane < HEAD_DIM, blk, jnp.where(lane == HEAD_DIM, 1.0, 0.0)).astype(BF16)
    store_heads(qr_ref, rope_cols(seg("qr"), None))
    store_heads(kr_ref, [blk * HEAD_DIM ** -0.5 for blk in rope_cols(seg("kr"), None)])
    vr_ref[0] = seg("vr").astype(BF16)


def _inproj(x, mul, shift, w_bf16, cos, sin, ones_blk, qg, kg, *, tm):
    bsz, s, d = x.shape
    n = w_bf16.shape[1]
    nt = s // tm
    const = lambda b, i: (0, 0)
    tok3 = lambda b, i: (b, i, 0)
    head4 = lambda b, i: (b, 0, i, 0)
    out_shape = [
        jax.ShapeDtypeStruct((bsz, ATTN_HEADS, s, HEAD_DIM), BF16),
        jax.ShapeDtypeStruct((bsz, ATTN_KV_HEADS, s, HEAD_DIM), BF16),
        jax.ShapeDtypeStruct((bsz, ATTN_KV_HEADS, s, LANES), BF16),
        jax.ShapeDtypeStruct((bsz, s, 512), BF16),
        jax.ShapeDtypeStruct((bsz, RET_HEADS, s, HEAD_DIM), BF16),
        jax.ShapeDtypeStruct((bsz, RET_HEADS, s, HEAD_DIM), BF16),
        jax.ShapeDtypeStruct((bsz, s, 512), BF16),
        jax.ShapeDtypeStruct((bsz, s, 512), BF16),
        jax.ShapeDtypeStruct((bsz, s, 2048), BF16),
    ]
    out_specs = [
        pl.BlockSpec((1, ATTN_HEADS, tm, HEAD_DIM), head4),
        pl.BlockSpec((1, ATTN_KV_HEADS, tm, HEAD_DIM), head4),
        pl.BlockSpec((1, ATTN_KV_HEADS, tm, LANES), head4),
        pl.BlockSpec((1, tm, 512), tok3),
        pl.BlockSpec((1, RET_HEADS, tm, HEAD_DIM), head4),
        pl.BlockSpec((1, RET_HEADS, tm, HEAD_DIM), head4),
        pl.BlockSpec((1, tm, 512), tok3),
        pl.BlockSpec((1, tm, 512), tok3),
        pl.BlockSpec((1, tm, 2048), tok3),
    ]
    return pl.pallas_call(
        functools.partial(_inproj_kernel, tm=tm),
        grid=(bsz, nt),
        in_specs=[pl.BlockSpec((1, tm, d), tok3),
                  pl.BlockSpec((1, 1, d), lambda b, i: (b, 0, 0)),
                  pl.BlockSpec((1, 1, d), lambda b, i: (b, 0, 0)),
                  pl.BlockSpec((d, n), const, pipeline_mode=pl.Buffered(1)),
                  pl.BlockSpec((tm, LANES), lambda b, i: (i, 0)),
                  pl.BlockSpec((tm, LANES), lambda b, i: (i, 0)),
                  pl.BlockSpec((512, 512), const),
                  pl.BlockSpec((1, 512), const),
                  pl.BlockSpec((1, LANES), const)],
        out_specs=out_specs,
        out_shape=out_shape,
        compiler_params=_params("arbitrary", "arbitrary"),
        name="in_proj",
    )(x, mul, shift, w_bf16, cos, sin, ones_blk, qg, kg)


def _attn_kernel(q_ref, k_ref, v_ref, z_ref, o_ref, *, tq, tk, seq):
    group = ATTN_HEADS // ATTN_KV_HEADS
    rows = group * tq
    q = q_ref[0].reshape(rows, HEAD_DIM)

    def body(j, carry):
        m, acc = carry
        start = pl.multiple_of(j * tk, tk)
        kc = k_ref[0, 0, pl.ds(start, tk), :]
        vc = v_ref[0, 0, pl.ds(start, tk), :]
        s = _mm_nt(q, kc)
        m_new = jnp.maximum(m, jnp.max(s, axis=1, keepdims=True))
        p = jnp.exp2(s - m_new)
        alpha = jnp.exp2(m - m_new)
        acc = alpha * acc + _mm(p.astype(BF16), vc)
        return m_new, acc

    m0 = jnp.full((rows, 1), -jnp.inf, F32)
    acc0 = jnp.zeros((rows, LANES), F32)
    _, acc = lax.fori_loop(0, seq // tk, body, (m0, acc0), unroll=True)
    out = acc[:, :HEAD_DIM] / acc[:, HEAD_DIM:HEAD_DIM + 1]
    out = jnp.concatenate([out[h * tq:(h + 1) * tq] for h in range(group)], axis=1)
    o_ref[0] = (out * z_ref[0].astype(F32)).astype(BF16)


def _attention(qa, ka, va, za, *, tq, tk):
    bsz, _, s, _ = qa.shape
    group = ATTN_HEADS // ATTN_KV_HEADS
    width = group * HEAD_DIM
    return pl.pallas_call(
        functools.partial(_attn_kernel, tq=tq, tk=tk, seq=s),
        grid=(bsz, ATTN_KV_HEADS, s // tq),
        in_specs=[pl.BlockSpec((1, group, tq, HEAD_DIM), lambda b, g, i: (b, g, i, 0)),
                  pl.BlockSpec((1, 1, s, HEAD_DIM), lambda b, g, i: (b, g, 0, 0)),
                  pl.BlockSpec((1, 1, s, LANES), lambda b, g, i: (b, g, 0, 0)),
                  pl.BlockSpec((1, tq, width), lambda b, g, i: (b, i, g))],
        out_specs=pl.BlockSpec((1, tq, width), lambda b, g, i: (b, i, g)),
        out_shape=jax.ShapeDtypeStruct((bsz, s, ATTN_HEADS * HEAD_DIM), BF16),
        compiler_params=_params("arbitrary", "arbitrary", "arbitrary"),
        name="gqa_attention",
    )(qa, ka, va, za)


def _log_sigmoid(w):
    return -(jnp.maximum(-w, 0.0) + jnp.log1p(jnp.exp(-jnp.abs(w))))


def _ret_kernel(q_ref, k_ref, v_ref, z_ref, wf_ref, wb_ref, gn_ref, o_ref, rb_ref, *, chunk, seq, unroll):
    nchunks = seq // chunk
    lg_f = _log_sigmoid(wf_ref[0])[0:1, 0:1]
    lg_b = _log_sigmoid(wb_ref[0])[0:1, 0:1]

    row = lax.broadcasted_iota(jnp.int32, (chunk, chunk), 0).astype(F32)
    colm = lax.broadcasted_iota(jnp.int32, (chunk, chunk), 1).astype(F32)
    diff = row - colm
    decay = jnp.where(diff >= 0, jnp.exp(lg_f * jnp.maximum(diff, 0.0)), jnp.exp(lg_b * jnp.maximum(-diff, 0.0)))
    idx = lax.broadcasted_iota(jnp.int32, (chunk, HEAD_DIM), 0).astype(F32)
    q_dec_f = jnp.exp(lg_f * (idx + 1.0))
    q_dec_b = jnp.exp(lg_b * (chunk - idx))
    k_dec_f = jnp.exp(lg_f * (chunk - 1.0 - idx))
    k_dec_b = jnp.exp(lg_b * idx)
    chunk_dec_f = jnp.exp(lg_f * chunk)
    chunk_dec_b = jnp.exp(lg_b * chunk)

    def load(ref, n):
        start = pl.multiple_of(n * chunk, chunk)
        return ref[0, 0, pl.ds(start, chunk), :]

    def load3(ref, n):
        start = pl.multiple_of(n * chunk, chunk)
        return ref[0, pl.ds(start, chunk), :]

    def bwd(t, state):
        n = nchunks - 1 - t
        rb_ref[n] = state
        kd = (load(k_ref, n).astype(F32) * k_dec_b).astype(BF16)
        return chunk_dec_b * state + _mm_tn(kd, load3(v_ref, n))

    lax.fori_loop(0, nchunks, bwd, jnp.zeros((HEAD_DIM, RET_VALUE_DIM), F32), unroll=unroll)

    gn = gn_ref[...]

    def fwd(n, state):
        q = load(q_ref, n)
        k = load(k_ref, n)
        v = load3(v_ref, n)
        qf = q.astype(F32)
        scores = (_mm_nt(q, k) * decay).astype(BF16)
        o = _mm(scores, v)
        o += _mm((qf * q_dec_f).astype(BF16), state.astype(BF16))
        o += _mm((qf * q_dec_b).astype(BF16), rb_ref[n].astype(BF16))
        mu = jnp.mean(o, axis=-1, keepdims=True)
        cen = o - mu
        var = jnp.mean(cen * cen, axis=-1, keepdims=True)
        y = cen * lax.rsqrt(var + EPS) * gn * load3(z_ref, n).astype(F32)
        start = pl.multiple_of(n * chunk, chunk)
        o_ref[0, pl.ds(start, chunk), :] = y.astype(BF16)
        kd = (k.astype(F32) * k_dec_f).astype(BF16)
        return chunk_dec_f * state + _mm_tn(kd, v)

    lax.fori_loop(0, nchunks, fwd, jnp.zeros((HEAD_DIM, RET_VALUE_DIM), F32), unroll=unroll)


def _retention(qr, kr, vr, zr, wf, wb, gn, *, chunk, unroll):
    bsz, heads, s, _ = qr.shape
    head_blk = lambda b, h: (b, h, 0, 0)
    col_blk = lambda b, h: (b, 0, h)
    return pl.pallas_call(
        functools.partial(_ret_kernel, chunk=chunk, seq=s, unroll=unroll),
        grid=(bsz, heads),
        in_specs=[pl.BlockSpec((1, 1, s, HEAD_DIM), head_blk),
                  pl.BlockSpec((1, 1, s, HEAD_DIM), head_blk),
                  pl.BlockSpec((1, s, RET_VALUE_DIM), col_blk),
                  pl.BlockSpec((1, s, RET_VALUE_DIM), col_blk),
                  pl.BlockSpec((1, 8, LANES), lambda b, h: (h, 0, 0)),
                  pl.BlockSpec((1, 8, LANES), lambda b, h: (h, 0, 0)),
                  pl.BlockSpec((1, RET_VALUE_DIM), lambda b, h: (0, h))],
        out_specs=pl.BlockSpec((1, s, RET_VALUE_DIM), col_blk),
        out_shape=jax.ShapeDtypeStruct((bsz, s, heads * RET_VALUE_DIM), BF16),
        scratch_shapes=[pltpu.VMEM((s // chunk, HEAD_DIM, RET_VALUE_DIM), F32)],
        compiler_params=_params("arbitrary", "arbitrary"),
        name="retention",
    )(qr, kr, vr, zr, wf, wb, gn)


def _outproj_kernel(ya_ref, yr_ref, gl_ref, x_ref, gate_ref, wpa_ref, wpr_ref, wout_ref, gpost_ref, o_ref, *, sub):
    d = x_ref.shape[-1]
    for r in range(0, x_ref.shape[1], sub):
        rows = pl.ds(r, sub)
        pa = _mm(ya_ref[0, rows, :], wpa_ref[...])
        pr = _mm(yr_ref[0, rows, :], wpr_ref[...])
        merged = gl_ref[0, rows, :d].astype(F32) * pa + gl_ref[0, rows, d:].astype(F32) * pr
        z = _mm(merged.astype(BF16), wout_ref[...])
        ms = jnp.mean(z * z, axis=-1, keepdims=True)
        y = z * lax.rsqrt(ms + EPS) * gpost_ref[...]
        o_ref[0, rows, :] = x_ref[0, rows, :] + gate_ref[0] * y


def _outproj(ya, yr, gl, x, gate, wpa, wpr, wout, gpost, *, tm, sub):
    bsz, s, d = x.shape
    tok3 = lambda b, i: (b, i, 0)
    const = lambda b, i: (0, 0)
    return pl.pallas_call(
        functools.partial(_outproj_kernel, sub=sub),
        grid=(bsz, s // tm),
        in_specs=[pl.BlockSpec((1, tm, ya.shape[-1]), tok3),
                  pl.BlockSpec((1, tm, yr.shape[-1]), tok3),
                  pl.BlockSpec((1, tm, 2 * d), tok3),
                  pl.BlockSpec((1, tm, d), tok3),
                  pl.BlockSpec((1, 1, d), lambda b, i: (b, 0, 0)),
                  pl.BlockSpec(wpa.shape, const),
                  pl.BlockSpec(wpr.shape, const),
                  pl.BlockSpec(wout.shape, const),
                  pl.BlockSpec((1, d), const)],
        out_specs=pl.BlockSpec((1, tm, d), tok3),
        out_shape=jax.ShapeDtypeStruct((bsz, s, d), F32),
        compiler_params=_params("arbitrary", "arbitrary"),
        name="out_proj",
    )(ya, yr, gl, x, gate, wpa, wpr, wout, gpost)


def _rope_tables(seq):
    t = jnp.arange(seq)
    row = (t // GRID_W).astype(F32)
    colp = (t % GRID_W).astype(F32)
    half = HEAD_DIM // 2
    inv_freq = ROPE_THETA ** (-jnp.arange(0, half, 2, dtype=F32) / half)
    ang_r = row[:, None] * inv_freq[None, :]
    ang_c = colp[:, None] * inv_freq[None, :]
    cos = jnp.concatenate([jnp.cos(ang_r)] * 2 + [jnp.cos(ang_c)] * 2, axis=-1)
    sin = jnp.concatenate([-jnp.sin(ang_r), jnp.sin(ang_r), -jnp.sin(ang_c), jnp.sin(ang_c)], axis=-1)
    return jnp.tile(cos, (1, 2)), jnp.tile(sin, (1, 2))


def _layer(x, mod, g_pre, w_in, qn_g, kn_g, w_dec_f, w_dec_b, gn_g, w_pa, w_pr, w_out, g_post, tables):
    bsz, s, d = x.shape
    shift, scale, gate = jnp.split(mod, 3, axis=-1)
    mul = (g_pre[None, :] * (1.0 + scale)).reshape(bsz, 1, d)
    shift = shift.reshape(bsz, 1, d)
    gate = gate.reshape(bsz, 1, d)

    blk = jnp.arange(512) // HEAD_DIM
    ones_blk = (blk[:, None] == blk[None, :]).astype(BF16)
    qg = jnp.tile(qn_g * (HEAD_DIM ** -0.5 * LOG2E), ATTN_HEADS).reshape(1, 512)
    kg = jnp.tile(kn_g, ATTN_KV_HEADS).reshape(1, LANES)
    cos, sin = tables

    qa, ka, va, za, qr, kr, vr, zr, gl = _inproj(x, mul, shift, w_in.astype(BF16), cos, sin, ones_blk, qg, kg, tm=1024)
    ya = _attention(qa, ka, va, za, tq=256, tk=s)
    wf = jnp.broadcast_to(w_dec_f.astype(F32)[:, None, None], (RET_HEADS, 8, LANES))
    wb = jnp.broadcast_to(w_dec_b.astype(F32)[:, None, None], (RET_HEADS, 8, LANES))
    yr = _retention(qr, kr, vr, zr, wf, wb, gn_g.reshape(1, -1), chunk=256, unroll=16)
    return _outproj(ya, yr, gl, x, gate, w_pa.astype(BF16), w_pr.astype(BF16), w_out.astype(BF16),
                    g_post.reshape(1, d), tm=1024, sub=512)


def kernel(x, c, w_ada, b_ada, g_pre, w_in, qn_g, kn_g, w_dec_f, w_dec_b, gn_g, w_pa, w_pr, w_out, g_post):
    tables = _rope_tables(x.shape[1])
    for l in range(w_ada.shape[0]):
        mod = _ada(c, w_ada[l], b_ada[l])
        x = _layer(x, mod, g_pre[l], w_in[l], qn_g[l], kn_g[l], w_dec_f[l], w_dec_b[l], gn_g[l],
                   w_pa[l], w_pr[l], w_out[l], g_post[l], tables)
    return x
```

```python
import functools
import math

import jax
import jax.numpy as jnp
from jax import lax
from jax.experimental import pallas as pl
from jax.experimental.pallas import tpu as pltpu

GRID_W = 64
ATTN_HEADS = 8
ATTN_KV_HEADS = 2
HEAD_DIM = 64
RET_HEADS = 4
RET_VALUE_DIM = 128
VT_ROWS = 80
ROPE_THETA = 10000.0
EPS = 1e-6

IN_SIZES = {"qa": 512, "ka": 128, "va": 128, "za": 512, "qr": 256, "kr": 256, "vr": 512, "zr": 512, "gl": 2048}
IN_OFFSETS = dict(zip(IN_SIZES, [sum(list(IN_SIZES.values())[:i]) for i in range(len(IN_SIZES))]))

LANES = 128
VMEM_LIMIT_BYTES = 56 * 1024 * 1024

F32 = jnp.float32
BF16 = jnp.bfloat16
LOG2E = math.log2(math.e)


def _mm(a, b):
    return jnp.dot(a, b, preferred_element_type=F32)


def _mm_nt(a, b):
    return lax.dot_general(a, b, (((1,), (1,)), ((), ())), preferred_element_type=F32)


def _mm_tn(a, b):
    return lax.dot_general(a, b, (((0,), (0,)), ((), ())), preferred_element_type=F32)


def _sigmoid(x):
    return 1.0 / (1.0 + jnp.exp(-x))


def _params(*semantics):
    return pltpu.CompilerParams(dimension_semantics=semantics, vmem_limit_bytes=VMEM_LIMIT_BYTES)


def _ada_kernel(c_ref, w_ref, b_ref, o_ref):
    c = c_ref[...]
    ca = (c * _sigmoid(c)).astype(BF16)
    o_ref[...] = _mm(ca, w_ref[...].astype(BF16)) + b_ref[...]


def _ada(c, w, b):
    bsz, d = c.shape
    n = w.shape[1]
    return pl.pallas_call(
        _ada_kernel,
        grid=(n // d,),
        in_specs=[pl.BlockSpec((bsz, d), lambda j: (0, 0)),
                  pl.BlockSpec((d, d), lambda j: (0, j)),
                  pl.BlockSpec((1, d), lambda j: (0, j))],
        out_specs=pl.BlockSpec((bsz, d), lambda j: (0, j)),
        out_shape=jax.ShapeDtypeStruct((bsz, n), F32),
        compiler_params=_params("arbitrary"),
        name="ada_mod",
    )(c, w, b.reshape(1, n))


def _rope(x, cos, sin, lo):
    partner = jnp.where(lo, pltpu.roll(x, LANES - 16, 1), pltpu.roll(x, 16, 1))
    return x * cos + partner * sin


def _head_rms(x, ones_blk):
    ms = _mm((x * x).astype(BF16), ones_blk) * (1.0 / HEAD_DIM)
    return x * lax.rsqrt(ms + EPS)


def _inproj_kernel(x_ref, mul_ref, shift_ref, w_ref, cos_ref, sin_ref, ones_ref, qg_ref, kg_ref,
                   qa_ref, ka_ref, va_ref, za_ref, qr_ref, kr_ref, vr_ref, zr_ref, gl_ref, *, tm):
    x = x_ref[0]
    ms = jnp.mean(x * x, axis=-1, keepdims=True)
    h = (x * lax.rsqrt(ms + EPS) * mul_ref[0] + shift_ref[0]).astype(BF16)

    cos = cos_ref[...]
    sin = sin_ref[...]
    lane = lax.broadcasted_iota(jnp.int32, (tm, LANES), 1)
    lo = (lane & 31) < 16
    ones_blk = ones_ref[...]

    def rope_cols(y, scale_row):
        outs = []
        for j in range(y.shape[1] // LANES):
            blk = y[:, j * LANES:(j + 1) * LANES]
            if scale_row is not None:
                blk = blk * scale_row[:, j * LANES:(j + 1) * LANES]
            outs.append(_rope(blk, cos, sin, lo))
        return outs

    def seg(name, part=0, width=None):
        start = IN_OFFSETS[name] + part
        width = IN_SIZES[name] if width is None else width
        return _mm(h, w_ref[:, start:start + width])

    def store_heads(ref, blks):
        for j, blk in enumerate(blks):
            blk = blk.astype(BF16)
            ref[0, 2 * j] = blk[:, :HEAD_DIM]
            ref[0, 2 * j + 1] = blk[:, HEAD_DIM:]

    for j in range(IN_SIZES["gl"] // 512):
        gl_ref[0, :, j * 512:(j + 1) * 512] = _sigmoid(seg("gl", j * 512, 512)).astype(BF16)
    z = seg("za")
    za_ref[0] = (z * _sigmoid(z)).astype(BF16)
    z = seg("zr")
    zr_ref[0] = (z * _sigmoid(z)).astype(BF16)
    q = _head_rms(seg("qa"), ones_blk)
    store_heads(qa_ref, rope_cols(q, qg_ref[...]))
    kv = seg("ka", 0, IN_SIZES["ka"] + IN_SIZES["va"])
    k = _head_rms(kv[:, :LANES], ones_blk[:LANES, :LANES])
    store_heads(ka_ref, rope_cols(k, kg_ref[...]))
    vt = kv[:, LANES:].T
    sub = lax.broadcasted_iota(jnp.int32, (VT_ROWS - HEAD_DIM, tm), 0)
    ones_rows = jnp.where(sub == 0, 1.0, 0.0)
    for g in range(ATTN_KV_HEADS):
        ext = jnp.concatenate([vt[g * HEAD_DIM:(g + 1) * HEAD_DIM], ones_rows], axis=0)
        va_ref[0, g] = ext.astype(BF16)
    store_heads(qr_ref, rope_cols(seg("qr"), None))
    store_heads(kr_ref, [blk * HEAD_DIM ** -0.5 for blk in rope_cols(seg("kr"), None)])
    vr_ref[0] = seg("vr").astype(BF16)


def _inproj(x, mul, shift, w_bf16, cos, sin, ones_blk, qg, kg, *, tm):
    bsz, s, d = x.shape
    n = w_bf16.shape[1]
    nt = s // tm
    const = lambda b, i: (0, 0)
    tok3 = lambda b, i: (b, i, 0)
    head4 = lambda b, i: (b, 0, i, 0)
    out_shape = [
        jax.ShapeDtypeStruct((bsz, ATTN_HEADS, s, HEAD_DIM), BF16),
        jax.ShapeDtypeStruct((bsz, ATTN_KV_HEADS, s, HEAD_DIM), BF16),
        jax.ShapeDtypeStruct((bsz, ATTN_KV_HEADS, VT_ROWS, s), BF16),
        jax.ShapeDtypeStruct((bsz, s, 512), BF16),
        jax.ShapeDtypeStruct((bsz, RET_HEADS, s, HEAD_DIM), BF16),
        jax.ShapeDtypeStruct((bsz, RET_HEADS, s, HEAD_DIM), BF16),
        jax.ShapeDtypeStruct((bsz, s, 512), BF16),
        jax.ShapeDtypeStruct((bsz, s, 512), BF16),
        jax.ShapeDtypeStruct((bsz, s, 2048), BF16),
    ]
    out_specs = [
        pl.BlockSpec((1, ATTN_HEADS, tm, HEAD_DIM), head4),
        pl.BlockSpec((1, ATTN_KV_HEADS, tm, HEAD_DIM), head4),
        pl.BlockSpec((1, ATTN_KV_HEADS, VT_ROWS, tm), lambda b, i: (b, 0, 0, i)),
        pl.BlockSpec((1, tm, 512), tok3),
        pl.BlockSpec((1, RET_HEADS, tm, HEAD_DIM), head4),
        pl.BlockSpec((1, RET_HEADS, tm, HEAD_DIM), head4),
        pl.BlockSpec((1, tm, 512), tok3),
        pl.BlockSpec((1, tm, 512), tok3),
        pl.BlockSpec((1, tm, 2048), tok3),
    ]
    return pl.pallas_call(
        functools.partial(_inproj_kernel, tm=tm),
        grid=(bsz, nt),
        in_specs=[pl.BlockSpec((1, tm, d), tok3),
                  pl.BlockSpec((1, 1, d), lambda b, i: (b, 0, 0)),
                  pl.BlockSpec((1, 1, d), lambda b, i: (b, 0, 0)),
                  pl.BlockSpec((d, n), const, pipeline_mode=pl.Buffered(1)),
                  pl.BlockSpec((tm, LANES), lambda b, i: (i, 0)),
                  pl.BlockSpec((tm, LANES), lambda b, i: (i, 0)),
                  pl.BlockSpec((512, 512), const),
                  pl.BlockSpec((1, 512), const),
                  pl.BlockSpec((1, LANES), const)],
        out_specs=out_specs,
        out_shape=out_shape,
        compiler_params=_params("arbitrary", "arbitrary"),
        name="in_proj",
    )(x, mul, shift, w_bf16, cos, sin, ones_blk, qg, kg)


def _attn_kernel(q_ref, k_ref, v_ref, z_ref, o_ref, *, tq, tk, seq, lookahead):
    group = ATTN_HEADS // ATTN_KV_HEADS
    cols = group * tq
    q = q_ref[0].reshape(cols, HEAD_DIM)

    def scores(j):
        return _mm_nt(k_ref[0, 0, j * tk:(j + 1) * tk, :], q)

    m = jnp.full((1, cols), -jnp.inf, F32)
    acc = jnp.zeros((VT_ROWS, cols), F32)
    nchunks = seq // tk
    pending = [scores(j) for j in range(min(lookahead, nchunks))]
    for j in range(nchunks):
        if j + lookahead < nchunks:
            pending.append(scores(j + lookahead))
        st = pending.pop(0)
        m_new = jnp.maximum(m, jnp.max(st, axis=0, keepdims=True))
        pt = jnp.exp2(st - m_new).astype(BF16)
        acc = jnp.exp2(m - m_new) * acc + _mm(v_ref[0, 0, :, j * tk:(j + 1) * tk], pt)
        m = m_new
    out = acc[:HEAD_DIM] / acc[HEAD_DIM:HEAD_DIM + 1]
    out = jnp.concatenate([out[:, h * tq:(h + 1) * tq].T for h in range(group)], axis=1)
    o_ref[0] = (out * z_ref[0].astype(F32)).astype(BF16)


def _attention(qa, ka, va, za, *, tq, tk, lookahead):
    bsz, _, s, _ = qa.shape
    group = ATTN_HEADS // ATTN_KV_HEADS
    width = group * HEAD_DIM
    return pl.pallas_call(
        functools.partial(_attn_kernel, tq=tq, tk=tk, seq=s, lookahead=lookahead),
        grid=(bsz, ATTN_KV_HEADS, s // tq),
        in_specs=[pl.BlockSpec((1, group, tq, HEAD_DIM), lambda b, g, i: (b, g, i, 0)),
                  pl.BlockSpec((1, 1, s, HEAD_DIM), lambda b, g, i: (b, g, 0, 0)),
                  pl.BlockSpec((1, 1, VT_ROWS, s), lambda b, g, i: (b, g, 0, 0)),
                  pl.BlockSpec((1, tq, width), lambda b, g, i: (b, i, g))],
        out_specs=pl.BlockSpec((1, tq, width), lambda b, g, i: (b, i, g)),
        out_shape=jax.ShapeDtypeStruct((bsz, s, ATTN_HEADS * HEAD_DIM), BF16),
        compiler_params=_params("arbitrary", "arbitrary", "arbitrary"),
        name="gqa_attention",
    )(qa, ka, va, za)


def _log_sigmoid(w):
    return -(jnp.maximum(-w, 0.0) + jnp.log1p(jnp.exp(-jnp.abs(w))))


def _ret_kernel(q_ref, k_ref, v_ref, z_ref, wf_ref, wb_ref, gn_ref, o_ref, rb_ref, *, chunk, seq, unroll):
    nchunks = seq // chunk
    lg_f = _log_sigmoid(wf_ref[0])[0:1, 0:1]
    lg_b = _log_sigmoid(wb_ref[0])[0:1, 0:1]

    row = lax.broadcasted_iota(jnp.int32, (chunk, chunk), 0).astype(F32)
    colm = lax.broadcasted_iota(jnp.int32, (chunk, chunk), 1).astype(F32)
    diff = row - colm
    decay = jnp.where(diff >= 0, jnp.exp(lg_f * jnp.maximum(diff, 0.0)), jnp.exp(lg_b * jnp.maximum(-diff, 0.0)))
    idx = lax.broadcasted_iota(jnp.int32, (chunk, HEAD_DIM), 0).astype(F32)
    q_dec_f = jnp.exp(lg_f * (idx + 1.0))
    q_dec_b = jnp.exp(lg_b * (chunk - idx))
    k_dec_f = jnp.exp(lg_f * (chunk - 1.0 - idx))
    k_dec_b = jnp.exp(lg_b * idx)
    chunk_dec_f = jnp.exp(lg_f * chunk)
    chunk_dec_b = jnp.exp(lg_b * chunk)

    def load(ref, n):
        start = pl.multiple_of(n * chunk, chunk)
        return ref[0, 0, pl.ds(start, chunk), :]

    def load3(ref, n):
        start = pl.multiple_of(n * chunk, chunk)
        return ref[0, pl.ds(start, chunk), :]

    def bwd(t, state):
        n = nchunks - 1 - t
        rb_ref[n] = state
        kd = (load(k_ref, n).astype(F32) * k_dec_b).astype(BF16)
        return chunk_dec_b * state + _mm_tn(kd, load3(v_ref, n))

    lax.fori_loop(0, nchunks, bwd, jnp.zeros((HEAD_DIM, RET_VALUE_DIM), F32), unroll=unroll)

    gn = gn_ref[...]

    def fwd(n, state):
        q = load(q_ref, n)
        k = load(k_ref, n)
        v = load3(v_ref, n)
        qf = q.astype(F32)
        scores = (_mm_nt(q, k) * decay).astype(BF16)
        o = _mm(scores, v)
        o += _mm((qf * q_dec_f).astype(BF16), state.astype(BF16))
        o += _mm((qf * q_dec_b).astype(BF16), rb_ref[n].astype(BF16))
        mu = jnp.mean(o, axis=-1, keepdims=True)
        cen = o - mu
        var = jnp.mean(cen * cen, axis=-1, keepdims=True)
        y = cen * lax.rsqrt(var + EPS) * gn * load3(z_ref, n).astype(F32)
        start = pl.multiple_of(n * chunk, chunk)
        o_ref[0, pl.ds(start, chunk), :] = y.astype(BF16)
        kd = (k.astype(F32) * k_dec_f).astype(BF16)
        return chunk_dec_f * state + _mm_tn(kd, v)

    lax.fori_loop(0, nchunks, fwd, jnp.zeros((HEAD_DIM, RET_VALUE_DIM), F32), unroll=unroll)


def _retention(qr, kr, vr, zr, wf, wb, gn, *, chunk, unroll):
    bsz, heads, s, _ = qr.shape
    head_blk = lambda b, h: (b, h, 0, 0)
    col_blk = lambda b, h: (b, 0, h)
    return pl.pallas_call(
        functools.partial(_ret_kernel, chunk=chunk, seq=s, unroll=unroll),
        grid=(bsz, heads),
        in_specs=[pl.BlockSpec((1, 1, s, HEAD_DIM), head_blk),
                  pl.BlockSpec((1, 1, s, HEAD_DIM), head_blk),
                  pl.BlockSpec((1, s, RET_VALUE_DIM), col_blk),
                  pl.BlockSpec((1, s, RET_VALUE_DIM), col_blk),
                  pl.BlockSpec((1, 8, LANES), lambda b, h: (h, 0, 0)),
                  pl.BlockSpec((1, 8, LANES), lambda b, h: (h, 0, 0)),
                  pl.BlockSpec((1, RET_VALUE_DIM), lambda b, h: (0, h))],
        out_specs=pl.BlockSpec((1, s, RET_VALUE_DIM), col_blk),
        out_shape=jax.ShapeDtypeStruct((bsz, s, heads * RET_VALUE_DIM), BF16),
        scratch_shapes=[pltpu.VMEM((s // chunk, HEAD_DIM, RET_VALUE_DIM), F32)],
        compiler_params=_params("arbitrary", "arbitrary"),
        name="retention",
    )(qr, kr, vr, zr, wf, wb, gn)


def _outproj_kernel(ya_ref, yr_ref, gl_ref, x_ref, gate_ref, wpa_ref, wpr_ref, wout_ref, gpost_ref, o_ref, *, sub):
    d = x_ref.shape[-1]
    for r in range(0, x_ref.shape[1], sub):
        rows = pl.ds(r, sub)
        pa = _mm(ya_ref[0, rows, :], wpa_ref[...])
        pr = _mm(yr_ref[0, rows, :], wpr_ref[...])
        merged = gl_ref[0, rows, :d].astype(F32) * pa + gl_ref[0, rows, d:].astype(F32) * pr
        z = _mm(merged.astype(BF16), wout_ref[...])
        ms = jnp.mean(z * z, axis=-1, keepdims=True)
        y = z * lax.rsqrt(ms + EPS) * gpost_ref[...]
        o_ref[0, rows, :] = x_ref[0, rows, :] + gate_ref[0] * y


def _outproj(ya, yr, gl, x, gate, wpa, wpr, wout, gpost, *, tm, sub):
    bsz, s, d = x.shape
    tok3 = lambda b, i: (b, i, 0)
    const = lambda b, i: (0, 0)
    return pl.pallas_call(
        functools.partial(_outproj_kernel, sub=sub),
        grid=(bsz, s // tm),
        in_specs=[pl.BlockSpec((1, tm, ya.shape[-1]), tok3),
                  pl.BlockSpec((1, tm, yr.shape[-1]), tok3),
                  pl.BlockSpec((1, tm, 2 * d), tok3),
                  pl.BlockSpec((1, tm, d), tok3),
                  pl.BlockSpec((1, 1, d), lambda b, i: (b, 0, 0)),
                  pl.BlockSpec(wpa.shape, const),
                  pl.BlockSpec(wpr.shape, const),
                  pl.BlockSpec(wout.shape, const),
                  pl.BlockSpec((1, d), const)],
        out_specs=pl.BlockSpec((1, tm, d), tok3),
        out_shape=jax.ShapeDtypeStruct((bsz, s, d), F32),
        compiler_params=_params("arbitrary", "arbitrary"),
        name="out_proj",
    )(ya, yr, gl, x, gate, wpa, wpr, wout, gpost)


def _rope_tables(seq):
    t = jnp.arange(seq)
    row = (t // GRID_W).astype(F32)
    colp = (t % GRID_W).astype(F32)
    half = HEAD_DIM // 2
    inv_freq = ROPE_THETA ** (-jnp.arange(0, half, 2, dtype=F32) / half)
    ang_r = row[:, None] * inv_freq[None, :]
    ang_c = colp[:, None] * inv_freq[None, :]
    cos = jnp.concatenate([jnp.cos(ang_r)] * 2 + [jnp.cos(ang_c)] * 2, axis=-1)
    sin = jnp.concatenate([-jnp.sin(ang_r), jnp.sin(ang_r), -jnp.sin(ang_c), jnp.sin(ang_c)], axis=-1)
    return jnp.tile(cos, (1, 2)), jnp.tile(sin, (1, 2))


def _layer(x, mod, g_pre, w_in, qn_g, kn_g, w_dec_f, w_dec_b, gn_g, w_pa, w_pr, w_out, g_post, tables):
    bsz, s, d = x.shape
    shift, scale, gate = jnp.split(mod, 3, axis=-1)
    mul = (g_pre[None, :] * (1.0 + scale)).reshape(bsz, 1, d)
    shift = shift.reshape(bsz, 1, d)
    gate = gate.reshape(bsz, 1, d)

    blk = jnp.arange(512) // HEAD_DIM
    ones_blk = (blk[:, None] == blk[None, :]).astype(BF16)
    qg = jnp.tile(qn_g * (HEAD_DIM ** -0.5 * LOG2E), ATTN_HEADS).reshape(1, 512)
    kg = jnp.tile(kn_g, ATTN_KV_HEADS).reshape(1, LANES)
    cos, sin = tables

    qa, ka, va, za, qr, kr, vr, zr, gl = _inproj(x, mul, shift, w_in.astype(BF16), cos, sin, ones_blk, qg, kg, tm=1024)
    ya = _attention(qa, ka, va, za, tq=512, tk=256, lookahead=1)
    wf = jnp.broadcast_to(w_dec_f.astype(F32)[:, None, None], (RET_HEADS, 8, LANES))
    wb = jnp.broadcast_to(w_dec_b.astype(F32)[:, None, None], (RET_HEADS, 8, LANES))
    yr = _retention(qr, kr, vr, zr, wf, wb, gn_g.reshape(1, -1), chunk=256, unroll=16)
    return _outproj(ya, yr, gl, x, gate, w_pa.astype(BF16), w_pr.astype(BF16), w_out.astype(BF16),
                    g_post.reshape(1, d), tm=1024, sub=512)


def kernel(x, c, w_ada, b_ada, g_pre, w_in, qn_g, kn_g, w_dec_f, w_dec_b, gn_g, w_pa, w_pr, w_out, g_post):
    tables = _rope_tables(x.shape[1])
    for l in range(w_ada.shape[0]):
        mod = _ada(c, w_ada[l], b_ada[l])
        x = _layer(x, mod, g_pre[l], w_in[l], qn_g[l], kn_g[l], w_dec_f[l], w_dec_b[l], gn_g[l],
                   w_pa[l], w_pr[l], w_out[l], g_post[l], tables)
    return x
```

```python
import functools
import math

import jax
import jax.numpy as jnp
from jax import lax
from jax.experimental import pallas as pl
from jax.experimental.pallas import tpu as pltpu

GRID_W = 64
ATTN_HEADS = 8
ATTN_KV_HEADS = 2
HEAD_DIM = 64
RET_HEADS = 4
RET_VALUE_DIM = 128
VT_ROWS = 80
ROPE_THETA = 10000.0
EPS = 1e-6

IN_SIZES = {"qa": 512, "ka": 128, "va": 128, "za": 512, "qr": 256, "kr": 256, "vr": 512, "zr": 512, "gl": 2048}
IN_OFFSETS = dict(zip(IN_SIZES, [sum(list(IN_SIZES.values())[:i]) for i in range(len(IN_SIZES))]))

LANES = 128
VMEM_LIMIT_BYTES = 56 * 1024 * 1024

F32 = jnp.float32
BF16 = jnp.bfloat16
LOG2E = math.log2(math.e)


def _mm(a, b):
    return jnp.dot(a, b, preferred_element_type=F32)


def _mm_nt(a, b):
    return lax.dot_general(a, b, (((1,), (1,)), ((), ())), preferred_element_type=F32)


def _mm_tn(a, b):
    return lax.dot_general(a, b, (((0,), (0,)), ((), ())), preferred_element_type=F32)


def _sigmoid(x):
    return 1.0 / (1.0 + jnp.exp(-x))


def _params(*semantics):
    return pltpu.CompilerParams(dimension_semantics=semantics, vmem_limit_bytes=VMEM_LIMIT_BYTES)


def _ada_kernel(c_ref, w_ref, b_ref, o_ref):
    c = c_ref[...]
    ca = (c * _sigmoid(c)).astype(BF16)
    o_ref[...] = _mm(ca, w_ref[...].astype(BF16)) + b_ref[...]


def _ada(c, w, b):
    bsz, d = c.shape
    n = w.shape[1]
    return pl.pallas_call(
        _ada_kernel,
        grid=(n // d,),
        in_specs=[pl.BlockSpec((bsz, d), lambda j: (0, 0)),
                  pl.BlockSpec((d, d), lambda j: (0, j)),
                  pl.BlockSpec((1, d), lambda j: (0, j))],
        out_specs=pl.BlockSpec((bsz, d), lambda j: (0, j)),
        out_shape=jax.ShapeDtypeStruct((bsz, n), F32),
        compiler_params=_params("arbitrary"),
        name="ada_mod",
    )(c, w, b.reshape(1, n))


def _rope(x, cos, sin, lo):
    partner = jnp.where(lo, pltpu.roll(x, LANES - 16, 1), pltpu.roll(x, 16, 1))
    return x * cos + partner * sin


def _head_rms(x, ones_blk):
    ms = _mm((x * x).astype(BF16), ones_blk) * (1.0 / HEAD_DIM)
    return x * lax.rsqrt(ms + EPS)


def _inproj_kernel(x_ref, mul_ref, shift_ref, w_ref, cos_ref, sin_ref, ones_ref, qg_ref, kg_ref,
                   qa_ref, ka_ref, va_ref, za_ref, qr_ref, kr_ref, vr_ref, zr_ref, gl_ref, *, tm):
    x = x_ref[0]
    ms = jnp.mean(x * x, axis=-1, keepdims=True)
    h = (x * lax.rsqrt(ms + EPS) * mul_ref[0] + shift_ref[0]).astype(BF16)

    cos = cos_ref[...]
    sin = sin_ref[...]
    lane = lax.broadcasted_iota(jnp.int32, (tm, LANES), 1)
    lo = (lane & 31) < 16
    ones_blk = ones_ref[...]

    def rope_cols(y, scale_row):
        outs = []
        for j in range(y.shape[1] // LANES):
            blk = y[:, j * LANES:(j + 1) * LANES]
            if scale_row is not None:
                blk = blk * scale_row[:, j * LANES:(j + 1) * LANES]
            outs.append(_rope(blk, cos, sin, lo))
        return outs

    def seg(name, part=0, width=None):
        start = IN_OFFSETS[name] + part
        width = IN_SIZES[name] if width is None else width
        return _mm(h, w_ref[:, start:start + width])

    def store_heads(ref, blks):
        for j, blk in enumerate(blks):
            blk = blk.astype(BF16)
            ref[0, 2 * j] = blk[:, :HEAD_DIM]
            ref[0, 2 * j + 1] = blk[:, HEAD_DIM:]

    for j in range(IN_SIZES["gl"] // 512):
        gl_ref[0, :, j * 512:(j + 1) * 512] = _sigmoid(seg("gl", j * 512, 512)).astype(BF16)
    z = seg("za")
    za_ref[0] = (z * _sigmoid(z)).astype(BF16)
    z = seg("zr")
    zr_ref[0] = (z * _sigmoid(z)).astype(BF16)
    q = _head_rms(seg("qa"), ones_blk)
    store_heads(qa_ref, rope_cols(q, qg_ref[...]))
    kv = seg("ka", 0, IN_SIZES["ka"] + IN_SIZES["va"])
    k = _head_rms(kv[:, :LANES], ones_blk[:LANES, :LANES])
    store_heads(ka_ref, rope_cols(k, kg_ref[...]))
    vt = kv[:, LANES:].T
    sub = lax.broadcasted_iota(jnp.int32, (VT_ROWS - HEAD_DIM, tm), 0)
    ones_rows = jnp.where(sub == 0, 1.0, 0.0)
    for g in range(ATTN_KV_HEADS):
        ext = jnp.concatenate([vt[g * HEAD_DIM:(g + 1) * HEAD_DIM], ones_rows], axis=0)
        va_ref[0, g] = ext.astype(BF16)
    store_heads(qr_ref, rope_cols(seg("qr"), None))
    store_heads(kr_ref, [blk * HEAD_DIM ** -0.5 for blk in rope_cols(seg("kr"), None)])
    vr_ref[0] = seg("vr").astype(BF16)


def _inproj(x, mul, shift, w_bf16, cos, sin, ones_blk, qg, kg, *, tm):
    bsz, s, d = x.shape
    n = w_bf16.shape[1]
    nt = s // tm
    const = lambda b, i: (0, 0)
    tok3 = lambda b, i: (b, i, 0)
    head4 = lambda b, i: (b, 0, i, 0)
    out_shape = [
        jax.ShapeDtypeStruct((bsz, ATTN_HEADS, s, HEAD_DIM), BF16),
        jax.ShapeDtypeStruct((bsz, ATTN_KV_HEADS, s, HEAD_DIM), BF16),
        jax.ShapeDtypeStruct((bsz, ATTN_KV_HEADS, VT_ROWS, s), BF16),
        jax.ShapeDtypeStruct((bsz, s, 512), BF16),
        jax.ShapeDtypeStruct((bsz, RET_HEADS, s, HEAD_DIM), BF16),
        jax.ShapeDtypeStruct((bsz, RET_HEADS, s, HEAD_DIM), BF16),
        jax.ShapeDtypeStruct((bsz, s, 512), BF16),
        jax.ShapeDtypeStruct((bsz, s, 512), BF16),
        jax.ShapeDtypeStruct((bsz, s, 2048), BF16),
    ]
    out_specs = [
        pl.BlockSpec((1, ATTN_HEADS, tm, HEAD_DIM), head4),
        pl.BlockSpec((1, ATTN_KV_HEADS, tm, HEAD_DIM), head4),
        pl.BlockSpec((1, ATTN_KV_HEADS, VT_ROWS, tm), lambda b, i: (b, 0, 0, i)),
        pl.BlockSpec((1, tm, 512), tok3),
        pl.BlockSpec((1, RET_HEADS, tm, HEAD_DIM), head4),
        pl.BlockSpec((1, RET_HEADS, tm, HEAD_DIM), head4),
        pl.BlockSpec((1, tm, 512), tok3),
        pl.BlockSpec((1, tm, 512), tok3),
        pl.BlockSpec((1, tm, 2048), tok3),
    ]
    return pl.pallas_call(
        functools.partial(_inproj_kernel, tm=tm),
        grid=(bsz, nt),
        in_specs=[pl.BlockSpec((1, tm, d), tok3),
                  pl.BlockSpec((1, 1, d), lambda b, i: (b, 0, 0)),
                  pl.BlockSpec((1, 1, d), lambda b, i: (b, 0, 0)),
                  pl.BlockSpec((d, n), const, pipeline_mode=pl.Buffered(1)),
                  pl.BlockSpec((tm, LANES), lambda b, i: (i, 0)),
                  pl.BlockSpec((tm, LANES), lambda b, i: (i, 0)),
                  pl.BlockSpec((512, 512), const),
                  pl.BlockSpec((1, 512), const),
                  pl.BlockSpec((1, LANES), const)],
        out_specs=out_specs,
        out_shape=out_shape,
        compiler_params=_params("arbitrary", "arbitrary"),
        name="in_proj",
    )(x, mul, shift, w_bf16, cos, sin, ones_blk, qg, kg)


def _attn_kernel(q_ref, k_ref, v_ref, z_ref, o_ref, *, tq, tk, cw, seq, lookahead):
    group = ATTN_HEADS // ATTN_KV_HEADS
    cols = group * tq
    q = q_ref[0].reshape(cols, HEAD_DIM)

    units = [(c, j) for j in range(seq // tk) for c in range(cols // cw)]

    def scores(c, j):
        return _mm_nt(k_ref[0, 0, j * tk:(j + 1) * tk, :], q[c * cw:(c + 1) * cw])

    m = [jnp.full((1, cw), -jnp.inf, F32)] * (cols // cw)
    acc = [jnp.zeros((VT_ROWS, cw), F32)] * (cols // cw)
    pending = [scores(*u) for u in units[:lookahead]]
    for i, (c, j) in enumerate(units):
        if i + lookahead < len(units):
            pending.append(scores(*units[i + lookahead]))
        st = pending.pop(0)
        m_new = jnp.maximum(m[c], jnp.max(st, axis=0, keepdims=True))
        pt = jnp.exp2(st - m_new).astype(BF16)
        acc[c] = jnp.exp2(m[c] - m_new) * acc[c] + _mm(v_ref[0, 0, :, j * tk:(j + 1) * tk], pt)
        m[c] = m_new
    acc = jnp.concatenate(acc, axis=1)
    out = acc[:HEAD_DIM] / acc[HEAD_DIM:HEAD_DIM + 1]
    out = jnp.concatenate([out[:, h * tq:(h + 1) * tq].T for h in range(group)], axis=1)
    o_ref[0] = (out * z_ref[0].astype(F32)).astype(BF16)


def _attention(qa, ka, va, za, *, tq, tk, cw, lookahead):
    bsz, _, s, _ = qa.shape
    group = ATTN_HEADS // ATTN_KV_HEADS
    width = group * HEAD_DIM
    return pl.pallas_call(
        functools.partial(_attn_kernel, tq=tq, tk=tk, cw=cw, seq=s, lookahead=lookahead),
        grid=(bsz, ATTN_KV_HEADS, s // tq),
        in_specs=[pl.BlockSpec((1, group, tq, HEAD_DIM), lambda b, g, i: (b, g, i, 0)),
                  pl.BlockSpec((1, 1, s, HEAD_DIM), lambda b, g, i: (b, g, 0, 0)),
                  pl.BlockSpec((1, 1, VT_ROWS, s), lambda b, g, i: (b, g, 0, 0)),
                  pl.BlockSpec((1, tq, width), lambda b, g, i: (b, i, g))],
        out_specs=pl.BlockSpec((1, tq, width), lambda b, g, i: (b, i, g)),
        out_shape=jax.ShapeDtypeStruct((bsz, s, ATTN_HEADS * HEAD_DIM), BF16),
        compiler_params=_params("arbitrary", "arbitrary", "arbitrary"),
        name="gqa_attention",
    )(qa, ka, va, za)


def _log_sigmoid(w):
    return -(jnp.maximum(-w, 0.0) + jnp.log1p(jnp.exp(-jnp.abs(w))))


def _ret_kernel(q_ref, k_ref, v_ref, z_ref, wf_ref, wb_ref, gn_ref, o_ref, rb_ref, *, chunk, seq, unroll):
    nchunks = seq // chunk
    lg_f = _log_sigmoid(wf_ref[0])[0:1, 0:1]
    lg_b = _log_sigmoid(wb_ref[0])[0:1, 0:1]

    row = lax.broadcasted_iota(jnp.int32, (chunk, chunk), 0).astype(F32)
    colm = lax.broadcasted_iota(jnp.int32, (chunk, chunk), 1).astype(F32)
    diff = row - colm
    decay = jnp.where(diff >= 0, jnp.exp(lg_f * jnp.maximum(diff, 0.0)), jnp.exp(lg_b * jnp.maximum(-diff, 0.0)))
    idx = lax.broadcasted_iota(jnp.int32, (chunk, HEAD_DIM), 0).astype(F32)
    q_dec_f = jnp.exp(lg_f * (idx + 1.0))
    q_dec_b = jnp.exp(lg_b * (chunk - idx))
    k_dec_f = jnp.exp(lg_f * (chunk - 1.0 - idx))
    k_dec_b = jnp.exp(lg_b * idx)
    chunk_dec_f = jnp.exp(lg_f * chunk)
    chunk_dec_b = jnp.exp(lg_b * chunk)

    def load(ref, n):
        start = pl.multiple_of(n * chunk, chunk)
        return ref[0, 0, pl.ds(start, chunk), :]

    def load3(ref, n):
        start = pl.multiple_of(n * chunk, chunk)
        return ref[0, pl.ds(start, chunk), :]

    def bwd(t, state):
        n = nchunks - 1 - t
        rb_ref[n] = state
        kd = (load(k_ref, n).astype(F32) * k_dec_b).astype(BF16)
        return chunk_dec_b * state + _mm_tn(kd, load3(v_ref, n))

    lax.fori_loop(0, nchunks, bwd, jnp.zeros((HEAD_DIM, RET_VALUE_DIM), F32), unroll=unroll)

    gn = gn_ref[...]

    def fwd(n, state):
        q = load(q_ref, n)
        k = load(k_ref, n)
        v = load3(v_ref, n)
        qf = q.astype(F32)
        scores = (_mm_nt(q, k) * decay).astype(BF16)
        o = _mm(scores, v)
        o += _mm((qf * q_dec_f).astype(BF16), state.astype(BF16))
        o += _mm((qf * q_dec_b).astype(BF16), rb_ref[n].astype(BF16))
        mu = jnp.mean(o, axis=-1, keepdims=True)
        cen = o - mu
        var = jnp.mean(cen * cen, axis=-1, keepdims=True)
        y = cen * lax.rsqrt(var + EPS) * gn * load3(z_ref, n).astype(F32)
        start = pl.multiple_of(n * chunk, chunk)
        o_ref[0, pl.ds(start, chunk), :] = y.astype(BF16)
        kd = (k.astype(F32) * k_dec_f).astype(BF16)
        return chunk_dec_f * state + _mm_tn(kd, v)

    lax.fori_loop(0, nchunks, fwd, jnp.zeros((HEAD_DIM, RET_VALUE_DIM), F32), unroll=unroll)


def _retention(qr, kr, vr, zr, wf, wb, gn, *, chunk, unroll):
    bsz, heads, s, _ = qr.shape
    head_blk = lambda b, h: (b, h, 0, 0)
    col_blk = lambda b, h: (b, 0, h)
    return pl.pallas_call(
        functools.partial(_ret_kernel, chunk=chunk, seq=s, unroll=unroll),
        grid=(bsz, heads),
        in_specs=[pl.BlockSpec((1, 1, s, HEAD_DIM), head_blk),
                  pl.BlockSpec((1, 1, s, HEAD_DIM), head_blk),
                  pl.BlockSpec((1, s, RET_VALUE_DIM), col_blk),
                  pl.BlockSpec((1, s, RET_VALUE_DIM), col_blk),
                  pl.BlockSpec((1, 8, LANES), lambda b, h: (h, 0, 0)),
                  pl.BlockSpec((1, 8, LANES), lambda b, h: (h, 0, 0)),
                  pl.BlockSpec((1, RET_VALUE_DIM), lambda b, h: (0, h))],
        out_specs=pl.BlockSpec((1, s, RET_VALUE_DIM), col_blk),
        out_shape=jax.ShapeDtypeStruct((bsz, s, heads * RET_VALUE_DIM), BF16),
        scratch_shapes=[pltpu.VMEM((s // chunk, HEAD_DIM, RET_VALUE_DIM), F32)],
        compiler_params=_params("arbitrary", "arbitrary"),
        name="retention",
    )(qr, kr, vr, zr, wf, wb, gn)


def _outproj_kernel(ya_ref, yr_ref, gl_ref, x_ref, gate_ref, wpa_ref, wpr_ref, wout_ref, gpost_ref, o_ref, *, sub):
    d = x_ref.shape[-1]
    for r in range(0, x_ref.shape[1], sub):
        rows = pl.ds(r, sub)
        pa = _mm(ya_ref[0, rows, :], wpa_ref[...])
        pr = _mm(yr_ref[0, rows, :], wpr_ref[...])
        merged = gl_ref[0, rows, :d].astype(F32) * pa + gl_ref[0, rows, d:].astype(F32) * pr
        z = _mm(merged.astype(BF16), wout_ref[...])
        ms = jnp.mean(z * z, axis=-1, keepdims=True)
        y = z * lax.rsqrt(ms + EPS) * gpost_ref[...]
        o_ref[0, rows, :] = x_ref[0, rows, :] + gate_ref[0] * y


def _outproj(ya, yr, gl, x, gate, wpa, wpr, wout, gpost, *, tm, sub):
    bsz, s, d = x.shape
    tok3 = lambda b, i: (b, i, 0)
    const = lambda b, i: (0, 0)
    return pl.pallas_call(
        functools.partial(_outproj_kernel, sub=sub),
        grid=(bsz, s // tm),
        in_specs=[pl.BlockSpec((1, tm, ya.shape[-1]), tok3),
                  pl.BlockSpec((1, tm, yr.shape[-1]), tok3),
                  pl.BlockSpec((1, tm, 2 * d), tok3),
                  pl.BlockSpec((1, tm, d), tok3),
                  pl.BlockSpec((1, 1, d), lambda b, i: (b, 0, 0)),
                  pl.BlockSpec(wpa.shape, const),
                  pl.BlockSpec(wpr.shape, const),
                  pl.BlockSpec(wout.shape, const),
                  pl.BlockSpec((1, d), const)],
        out_specs=pl.BlockSpec((1, tm, d), tok3),
        out_shape=jax.ShapeDtypeStruct((bsz, s, d), F32),
        compiler_params=_params("arbitrary", "arbitrary"),
        name="out_proj",
    )(ya, yr, gl, x, gate, wpa, wpr, wout, gpost)


def _rope_tables(seq):
    t = jnp.arange(seq)
    row = (t // GRID_W).astype(F32)
    colp = (t % GRID_W).astype(F32)
    half = HEAD_DIM // 2
    inv_freq = ROPE_THETA ** (-jnp.arange(0, half, 2, dtype=F32) / half)
    ang_r = row[:, None] * inv_freq[None, :]
    ang_c = colp[:, None] * inv_freq[None, :]
    cos = jnp.concatenate([jnp.cos(ang_r)] * 2 + [jnp.cos(ang_c)] * 2, axis=-1)
    sin = jnp.concatenate([-jnp.sin(ang_r), jnp.sin(ang_r), -jnp.sin(ang_c), jnp.sin(ang_c)], axis=-1)
    return jnp.tile(cos, (1, 2)), jnp.tile(sin, (1, 2))


def _layer(x, mod, g_pre, w_in, qn_g, kn_g, w_dec_f, w_dec_b, gn_g, w_pa, w_pr, w_out, g_post, tables):
    bsz, s, d = x.shape
    shift, scale, gate = jnp.split(mod, 3, axis=-1)
    mul = (g_pre[None, :] * (1.0 + scale)).reshape(bsz, 1, d)
    shift = shift.reshape(bsz, 1, d)
    gate = gate.reshape(bsz, 1, d)

    blk = jnp.arange(512) // HEAD_DIM
    ones_blk = (blk[:, None] == blk[None, :]).astype(BF16)
    qg = jnp.tile(qn_g * (HEAD_DIM ** -0.5 * LOG2E), ATTN_HEADS).reshape(1, 512)
    kg = jnp.tile(kn_g, ATTN_KV_HEADS).reshape(1, LANES)
    cos, sin = tables

    qa, ka, va, za, qr, kr, vr, zr, gl = _inproj(x, mul, shift, w_in.astype(BF16), cos, sin, ones_blk, qg, kg, tm=1024)
    ya = _attention(qa, ka, va, za, tq=512, tk=256, cw=256, lookahead=8)
    wf = jnp.broadcast_to(w_dec_f.astype(F32)[:, None, None], (RET_HEADS, 8, LANES))
    wb = jnp.broadcast_to(w_dec_b.astype(F32)[:, None, None], (RET_HEADS, 8, LANES))
    yr = _retention(qr, kr, vr, zr, wf, wb, gn_g.reshape(1, -1), chunk=256, unroll=16)
    return _outproj(ya, yr, gl, x, gate, w_pa.astype(BF16), w_pr.astype(BF16), w_out.astype(BF16),
                    g_post.reshape(1, d), tm=1024, sub=512)


def kernel(x, c, w_ada, b_ada, g_pre, w_in, qn_g, kn_g, w_dec_f, w_dec_b, gn_g, w_pa, w_pr, w_out, g_post):
    tables = _rope_tables(x.shape[1])
    for l in range(w_ada.shape[0]):
        mod = _ada(c, w_ada[l], b_ada[l])
        x = _layer(x, mod, g_pre[l], w_in[l], qn_g[l], kn_g[l], w_dec_f[l], w_dec_b[l], gn_g[l],
                   w_pa[l], w_pr[l], w_out[l], g_post[l], tables)
    return x
```

```python
import functools
import math

import jax
import jax.numpy as jnp
from jax import lax
from jax.experimental import pallas as pl
from jax.experimental.pallas import tpu as pltpu

GRID_W = 64
ATTN_HEADS = 8
ATTN_KV_HEADS = 2
HEAD_DIM = 64
RET_HEADS = 4
RET_VALUE_DIM = 128
VT_ROWS = 80
ROPE_THETA = 10000.0
EPS = 1e-6

IN_SIZES = {"qa": 512, "ka": 128, "va": 128, "za": 512, "qr": 256, "kr": 256, "vr": 512, "zr": 512, "gl": 2048}
IN_OFFSETS = dict(zip(IN_SIZES, [sum(list(IN_SIZES.values())[:i]) for i in range(len(IN_SIZES))]))

LANES = 128
VMEM_LIMIT_BYTES = 56 * 1024 * 1024

F32 = jnp.float32
BF16 = jnp.bfloat16
LOG2E = math.log2(math.e)


def _mm(a, b):
    return jnp.dot(a, b, preferred_element_type=F32)


def _mm_nt(a, b):
    return lax.dot_general(a, b, (((1,), (1,)), ((), ())), preferred_element_type=F32)


def _mm_tn(a, b):
    return lax.dot_general(a, b, (((0,), (0,)), ((), ())), preferred_element_type=F32)


def _sigmoid(x):
    return 1.0 / (1.0 + jnp.exp(-x))


def _params(*semantics):
    return pltpu.CompilerParams(dimension_semantics=semantics, vmem_limit_bytes=VMEM_LIMIT_BYTES)


def _ada_kernel(c_ref, w_ref, b_ref, o_ref):
    c = c_ref[...]
    ca = (c * _sigmoid(c)).astype(BF16)
    o_ref[...] = _mm(ca, w_ref[...].astype(BF16)) + b_ref[...]


def _ada(c, w, b):
    bsz, d = c.shape
    n = w.shape[1]
    return pl.pallas_call(
        _ada_kernel,
        grid=(n // d,),
        in_specs=[pl.BlockSpec((bsz, d), lambda j: (0, 0)),
                  pl.BlockSpec((d, d), lambda j: (0, j)),
                  pl.BlockSpec((1, d), lambda j: (0, j))],
        out_specs=pl.BlockSpec((bsz, d), lambda j: (0, j)),
        out_shape=jax.ShapeDtypeStruct((bsz, n), F32),
        compiler_params=_params("arbitrary"),
        name="ada_mod",
    )(c, w, b.reshape(1, n))


def _rope(x, cos, sin, lo):
    partner = jnp.where(lo, pltpu.roll(x, LANES - 16, 1), pltpu.roll(x, 16, 1))
    return x * cos + partner * sin


def _head_rms(x, ones_blk):
    ms = _mm((x * x).astype(BF16), ones_blk) * (1.0 / HEAD_DIM)
    return x * lax.rsqrt(ms + EPS)


def _inproj_kernel(x_ref, mul_ref, shift_ref, w_ref, cos_ref, sin_ref, ones_ref, qg_ref, kg_ref,
                   qa_ref, ka_ref, va_ref, za_ref, qr_ref, kr_ref, vr_ref, zr_ref, gl_ref, *, tm):
    x = x_ref[0]
    ms = jnp.mean(x * x, axis=-1, keepdims=True)
    h = (x * lax.rsqrt(ms + EPS) * mul_ref[0] + shift_ref[0]).astype(BF16)

    cos = cos_ref[...]
    sin = sin_ref[...]
    lane = lax.broadcasted_iota(jnp.int32, (tm, LANES), 1)
    lo = (lane & 31) < 16
    ones_blk = ones_ref[...]

    def rope_cols(y, scale_row):
        outs = []
        for j in range(y.shape[1] // LANES):
            blk = y[:, j * LANES:(j + 1) * LANES]
            if scale_row is not None:
                blk = blk * scale_row[:, j * LANES:(j + 1) * LANES]
            outs.append(_rope(blk, cos, sin, lo))
        return outs

    def seg(name, part=0, width=None):
        start = IN_OFFSETS[name] + part
        width = IN_SIZES[name] if width is None else width
        return _mm(h, w_ref[:, start:start + width])

    def store_heads(ref, blks):
        for j, blk in enumerate(blks):
            blk = blk.astype(BF16)
            ref[0, 2 * j] = blk[:, :HEAD_DIM]
            ref[0, 2 * j + 1] = blk[:, HEAD_DIM:]

    for j in range(IN_SIZES["gl"] // 512):
        gl_ref[0, :, j * 512:(j + 1) * 512] = _sigmoid(seg("gl", j * 512, 512)).astype(BF16)
    z = seg("za")
    za_ref[0] = (z * _sigmoid(z)).astype(BF16)
    z = seg("zr")
    zr_ref[0] = (z * _sigmoid(z)).astype(BF16)
    q = _head_rms(seg("qa"), ones_blk)
    store_heads(qa_ref, rope_cols(q, qg_ref[...]))
    kv = seg("ka", 0, IN_SIZES["ka"] + IN_SIZES["va"])
    k = _head_rms(kv[:, :LANES], ones_blk[:LANES, :LANES])
    store_heads(ka_ref, rope_cols(k, kg_ref[...]))
    vt = kv[:, LANES:].T
    sub = lax.broadcasted_iota(jnp.int32, (VT_ROWS - HEAD_DIM, tm), 0)
    ones_rows = jnp.where(sub == 0, 1.0, 0.0)
    for g in range(ATTN_KV_HEADS):
        ext = jnp.concatenate([vt[g * HEAD_DIM:(g + 1) * HEAD_DIM], ones_rows], axis=0)
        va_ref[0, g] = ext.astype(BF16)
    store_heads(qr_ref, rope_cols(seg("qr"), None))
    store_heads(kr_ref, [blk * HEAD_DIM ** -0.5 for blk in rope_cols(seg("kr"), None)])
    vr_ref[0] = seg("vr").astype(BF16)


def _inproj(x, mul, shift, w_bf16, cos, sin, ones_blk, qg, kg, *, tm):
    bsz, s, d = x.shape
    n = w_bf16.shape[1]
    nt = s // tm
    const = lambda b, i: (0, 0)
    tok3 = lambda b, i: (b, i, 0)
    head4 = lambda b, i: (b, 0, i, 0)
    out_shape = [
        jax.ShapeDtypeStruct((bsz, ATTN_HEADS, s, HEAD_DIM), BF16),
        jax.ShapeDtypeStruct((bsz, ATTN_KV_HEADS, s, HEAD_DIM), BF16),
        jax.ShapeDtypeStruct((bsz, ATTN_KV_HEADS, VT_ROWS, s), BF16),
        jax.ShapeDtypeStruct((bsz, s, 512), BF16),
        jax.ShapeDtypeStruct((bsz, RET_HEADS, s, HEAD_DIM), BF16),
        jax.ShapeDtypeStruct((bsz, RET_HEADS, s, HEAD_DIM), BF16),
        jax.ShapeDtypeStruct((bsz, s, 512), BF16),
        jax.ShapeDtypeStruct((bsz, s, 512), BF16),
        jax.ShapeDtypeStruct((bsz, s, 2048), BF16),
    ]
    out_specs = [
        pl.BlockSpec((1, ATTN_HEADS, tm, HEAD_DIM), head4),
        pl.BlockSpec((1, ATTN_KV_HEADS, tm, HEAD_DIM), head4),
        pl.BlockSpec((1, ATTN_KV_HEADS, VT_ROWS, tm), lambda b, i: (b, 0, 0, i)),
        pl.BlockSpec((1, tm, 512), tok3),
        pl.BlockSpec((1, RET_HEADS, tm, HEAD_DIM), head4),
        pl.BlockSpec((1, RET_HEADS, tm, HEAD_DIM), head4),
        pl.BlockSpec((1, tm, 512), tok3),
        pl.BlockSpec((1, tm, 512), tok3),
        pl.BlockSpec((1, tm, 2048), tok3),
    ]
    return pl.pallas_call(
        functools.partial(_inproj_kernel, tm=tm),
        grid=(bsz, nt),
        in_specs=[pl.BlockSpec((1, tm, d), tok3),
                  pl.BlockSpec((1, 1, d), lambda b, i: (b, 0, 0)),
                  pl.BlockSpec((1, 1, d), lambda b, i: (b, 0, 0)),
                  pl.BlockSpec((d, n), const, pipeline_mode=pl.Buffered(1)),
                  pl.BlockSpec((tm, LANES), lambda b, i: (i, 0)),
                  pl.BlockSpec((tm, LANES), lambda b, i: (i, 0)),
                  pl.BlockSpec((512, 512), const),
                  pl.BlockSpec((1, 512), const),
                  pl.BlockSpec((1, LANES), const)],
        out_specs=out_specs,
        out_shape=out_shape,
        compiler_params=_params("arbitrary", "arbitrary"),
        name="in_proj",
    )(x, mul, shift, w_bf16, cos, sin, ones_blk, qg, kg)


def _attn_kernel(q_ref, k_ref, v_ref, z_ref, o_ref, *, tq, tk, cw, seq, lookahead):
    group = ATTN_HEADS // ATTN_KV_HEADS
    cols = group * tq
    q = q_ref[0].reshape(cols, HEAD_DIM)

    units = [(c, j) for j in range(seq // tk) for c in range(cols // cw)]

    def scores(c, j):
        return _mm_nt(k_ref[0, 0, j * tk:(j + 1) * tk, :], q[c * cw:(c + 1) * cw])

    m = [jnp.full((1, cw), -jnp.inf, F32)] * (cols // cw)
    acc = [jnp.zeros((VT_ROWS, cw), F32)] * (cols // cw)
    pending = [scores(*u) for u in units[:lookahead]]
    for i, (c, j) in enumerate(units):
        if i + lookahead < len(units):
            pending.append(scores(*units[i + lookahead]))
        st = pending.pop(0)
        m_new = jnp.maximum(m[c], jnp.max(st, axis=0, keepdims=True))
        pt = jnp.exp2(st - m_new).astype(BF16)
        acc[c] = jnp.exp2(m[c] - m_new) * acc[c] + _mm(v_ref[0, 0, :, j * tk:(j + 1) * tk], pt)
        m[c] = m_new
    acc = jnp.concatenate(acc, axis=1)
    out = acc[:HEAD_DIM] / acc[HEAD_DIM:HEAD_DIM + 1]
    out = jnp.concatenate([out[:, h * tq:(h + 1) * tq].T for h in range(group)], axis=1)
    o_ref[0] = (out * z_ref[0].astype(F32)).astype(BF16)


def _attention(qa, ka, va, za, *, tq, tk, cw, lookahead):
    bsz, _, s, _ = qa.shape
    group = ATTN_HEADS // ATTN_KV_HEADS
    width = group * HEAD_DIM
    return pl.pallas_call(
        functools.partial(_attn_kernel, tq=tq, tk=tk, cw=cw, seq=s, lookahead=lookahead),
        grid=(bsz, ATTN_KV_HEADS, s // tq),
        in_specs=[pl.BlockSpec((1, group, tq, HEAD_DIM), lambda b, g, i: (b, g, i, 0)),
                  pl.BlockSpec((1, 1, s, HEAD_DIM), lambda b, g, i: (b, g, 0, 0)),
                  pl.BlockSpec((1, 1, VT_ROWS, s), lambda b, g, i: (b, g, 0, 0)),
                  pl.BlockSpec((1, tq, width), lambda b, g, i: (b, i, g))],
        out_specs=pl.BlockSpec((1, tq, width), lambda b, g, i: (b, i, g)),
        out_shape=jax.ShapeDtypeStruct((bsz, s, ATTN_HEADS * HEAD_DIM), BF16),
        compiler_params=_params("arbitrary", "arbitrary", "arbitrary"),
        name="gqa_attention",
    )(qa, ka, va, za)


def _log_sigmoid(w):
    return -(jnp.maximum(-w, 0.0) + jnp.log1p(jnp.exp(-jnp.abs(w))))


def _ret_kernel(q_ref, k_ref, v_ref, z_ref, wf_ref, wb_ref, gn_ref, o_ref, *, chunk, seq, lookahead):
    nchunks = seq // chunk
    lg_f = _log_sigmoid(wf_ref[0])[0:1, 0:1]
    lg_b = _log_sigmoid(wb_ref[0])[0:1, 0:1]

    row = lax.broadcasted_iota(jnp.int32, (chunk, chunk), 0).astype(F32)
    colm = lax.broadcasted_iota(jnp.int32, (chunk, chunk), 1).astype(F32)
    diff = row - colm
    decay = jnp.where(diff >= 0, jnp.exp(lg_f * jnp.maximum(diff, 0.0)), jnp.exp(lg_b * jnp.maximum(-diff, 0.0)))
    idx = lax.broadcasted_iota(jnp.int32, (chunk, HEAD_DIM), 0).astype(F32)
    q_dec_f = jnp.exp(lg_f * (idx + 1.0))
    q_dec_b = jnp.exp(lg_b * (chunk - idx))
    k_dec_f = jnp.exp(lg_f * (chunk - 1.0 - idx))
    k_dec_b = jnp.exp(lg_b * idx)
    chunk_dec_f = jnp.exp(lg_f * chunk)
    chunk_dec_b = jnp.exp(lg_b * chunk)

    def rows(n):
        return slice(n * chunk, (n + 1) * chunk)

    def kv_update(n, k_dec):
        kd = (k_ref[0, 0, rows(n), :].astype(F32) * k_dec).astype(BF16)
        return _mm_tn(kd, v_ref[0, rows(n), :])

    def scan(order, k_dec, chunk_dec):
        updates = {n: kv_update(n, k_dec) for n in order}
        states, state = {}, jnp.zeros((HEAD_DIM, RET_VALUE_DIM), F32)
        for n in order:
            states[n] = state.astype(BF16)
            state = chunk_dec * state + updates[n]
        return states

    state_b = scan(range(nchunks - 1, -1, -1), k_dec_b, chunk_dec_b)
    state_f = scan(range(nchunks), k_dec_f, chunk_dec_f)

    def scores(n):
        return _mm_nt(q_ref[0, 0, rows(n), :], k_ref[0, 0, rows(n), :])

    gn = gn_ref[...]
    pending = [scores(n) for n in range(min(lookahead, nchunks))]
    for n in range(nchunks):
        if n + lookahead < nchunks:
            pending.append(scores(n + lookahead))
        qf = q_ref[0, 0, rows(n), :].astype(F32)
        o = _mm((pending.pop(0) * decay).astype(BF16), v_ref[0, rows(n), :])
        o += _mm((qf * q_dec_f).astype(BF16), state_f[n])
        o += _mm((qf * q_dec_b).astype(BF16), state_b[n])
        mu = jnp.mean(o, axis=-1, keepdims=True)
        cen = o - mu
        var = jnp.mean(cen * cen, axis=-1, keepdims=True)
        y = cen * lax.rsqrt(var + EPS) * gn * z_ref[0, rows(n), :].astype(F32)
        o_ref[0, rows(n), :] = y.astype(BF16)


def _retention(qr, kr, vr, zr, wf, wb, gn, *, chunk, lookahead):
    bsz, heads, s, _ = qr.shape
    head_blk = lambda b, h: (b, h, 0, 0)
    col_blk = lambda b, h: (b, 0, h)
    return pl.pallas_call(
        functools.partial(_ret_kernel, chunk=chunk, seq=s, lookahead=lookahead),
        grid=(bsz, heads),
        in_specs=[pl.BlockSpec((1, 1, s, HEAD_DIM), head_blk),
                  pl.BlockSpec((1, 1, s, HEAD_DIM), head_blk),
                  pl.BlockSpec((1, s, RET_VALUE_DIM), col_blk),
                  pl.BlockSpec((1, s, RET_VALUE_DIM), col_blk),
                  pl.BlockSpec((1, 8, LANES), lambda b, h: (h, 0, 0)),
                  pl.BlockSpec((1, 8, LANES), lambda b, h: (h, 0, 0)),
                  pl.BlockSpec((1, RET_VALUE_DIM), lambda b, h: (0, h))],
        out_specs=pl.BlockSpec((1, s, RET_VALUE_DIM), col_blk),
        out_shape=jax.ShapeDtypeStruct((bsz, s, heads * RET_VALUE_DIM), BF16),
        compiler_params=_params("arbitrary", "arbitrary"),
        name="retention",
    )(qr, kr, vr, zr, wf, wb, gn)


def _outproj_kernel(ya_ref, yr_ref, gl_ref, x_ref, gate_ref, wpa_ref, wpr_ref, wout_ref, gpost_ref, o_ref, *, sub):
    d = x_ref.shape[-1]
    tiles = [pl.ds(r, sub) for r in range(0, x_ref.shape[1], sub)]
    branch = [(_mm(ya_ref[0, rows, :], wpa_ref[...]), _mm(yr_ref[0, rows, :], wpr_ref[...])) for rows in tiles]
    for rows, (pa, pr) in zip(tiles, branch):
        merged = gl_ref[0, rows, :d].astype(F32) * pa + gl_ref[0, rows, d:].astype(F32) * pr
        z = _mm(merged.astype(BF16), wout_ref[...])
        ms = jnp.mean(z * z, axis=-1, keepdims=True)
        y = z * lax.rsqrt(ms + EPS) * gpost_ref[...]
        o_ref[0, rows, :] = x_ref[0, rows, :] + gate_ref[0] * y


def _outproj(ya, yr, gl, x, gate, wpa, wpr, wout, gpost, *, tm, sub):
    bsz, s, d = x.shape
    tok3 = lambda b, i: (b, i, 0)
    const = lambda b, i: (0, 0)
    return pl.pallas_call(
        functools.partial(_outproj_kernel, sub=sub),
        grid=(bsz, s // tm),
        in_specs=[pl.BlockSpec((1, tm, ya.shape[-1]), tok3),
                  pl.BlockSpec((1, tm, yr.shape[-1]), tok3),
                  pl.BlockSpec((1, tm, 2 * d), tok3),
                  pl.BlockSpec((1, tm, d), tok3),
                  pl.BlockSpec((1, 1, d), lambda b, i: (b, 0, 0)),
                  pl.BlockSpec(wpa.shape, const),
                  pl.BlockSpec(wpr.shape, const),
                  pl.BlockSpec(wout.shape, const),
                  pl.BlockSpec((1, d), const)],
        out_specs=pl.BlockSpec((1, tm, d), tok3),
        out_shape=jax.ShapeDtypeStruct((bsz, s, d), F32),
        compiler_params=_params("arbitrary", "arbitrary"),
        name="out_proj",
    )(ya, yr, gl, x, gate, wpa, wpr, wout, gpost)


def _rope_tables(seq):
    t = jnp.arange(seq)
    row = (t // GRID_W).astype(F32)
    colp = (t % GRID_W).astype(F32)
    half = HEAD_DIM // 2
    inv_freq = ROPE_THETA ** (-jnp.arange(0, half, 2, dtype=F32) / half)
    ang_r = row[:, None] * inv_freq[None, :]
    ang_c = colp[:, None] * inv_freq[None, :]
    cos = jnp.concatenate([jnp.cos(ang_r)] * 2 + [jnp.cos(ang_c)] * 2, axis=-1)
    sin = jnp.concatenate([-jnp.sin(ang_r), jnp.sin(ang_r), -jnp.sin(ang_c), jnp.sin(ang_c)], axis=-1)
    return jnp.tile(cos, (1, 2)), jnp.tile(sin, (1, 2))


def _layer(x, mod, g_pre, w_in, qn_g, kn_g, w_dec_f, w_dec_b, gn_g, w_pa, w_pr, w_out, g_post, tables):
    bsz, s, d = x.shape
    shift, scale, gate = jnp.split(mod, 3, axis=-1)
    mul = (g_pre[None, :] * (1.0 + scale)).reshape(bsz, 1, d)
    shift = shift.reshape(bsz, 1, d)
    gate = gate.reshape(bsz, 1, d)

    blk = jnp.arange(512) // HEAD_DIM
    ones_blk = (blk[:, None] == blk[None, :]).astype(BF16)
    qg = jnp.tile(qn_g * (HEAD_DIM ** -0.5 * LOG2E), ATTN_HEADS).reshape(1, 512)
    kg = jnp.tile(kn_g, ATTN_KV_HEADS).reshape(1, LANES)
    cos, sin = tables

    qa, ka, va, za, qr, kr, vr, zr, gl = _inproj(x, mul, shift, w_in.astype(BF16), cos, sin, ones_blk, qg, kg, tm=1024)
    ya = _attention(qa, ka, va, za, tq=512, tk=256, cw=256, lookahead=8)
    wf = jnp.broadcast_to(w_dec_f.astype(F32)[:, None, None], (RET_HEADS, 8, LANES))
    wb = jnp.broadcast_to(w_dec_b.astype(F32)[:, None, None], (RET_HEADS, 8, LANES))
    yr = _retention(qr, kr, vr, zr, wf, wb, gn_g.reshape(1, -1), chunk=256, lookahead=3)
    return _outproj(ya, yr, gl, x, gate, w_pa.astype(BF16), w_pr.astype(BF16), w_out.astype(BF16),
                    g_post.reshape(1, d), tm=1024, sub=512)


def kernel(x, c, w_ada, b_ada, g_pre, w_in, qn_g, kn_g, w_dec_f, w_dec_b, gn_g, w_pa, w_pr, w_out, g_post):
    tables = _rope_tables(x.shape[1])
    for l in range(w_ada.shape[0]):
        mod = _ada(c, w_ada[l], b_ada[l])
        x = _layer(x, mod, g_pre[l], w_in[l], qn_g[l], kn_g[l], w_dec_f[l], w_dec_b[l], gn_g[l],
                   w_pa[l], w_pr[l], w_out[l], g_post[l], tables)
    return x
```

```python
import functools
import math

import jax
import jax.numpy as jnp
from jax import lax
from jax.experimental import pallas as pl
from jax.experimental.pallas import tpu as pltpu

GRID_W = 64
ATTN_HEADS = 8
ATTN_KV_HEADS = 2
HEAD_DIM = 64
RET_HEADS = 4
RET_VALUE_DIM = 128
VT_ROWS = 80
ROPE_THETA = 10000.0
EPS = 1e-6

IN_SIZES = {"qa": 512, "ka": 128, "va": 128, "za": 512, "qr": 256, "kr": 256, "vr": 512, "zr": 512, "gl": 2048}
IN_OFFSETS = dict(zip(IN_SIZES, [sum(list(IN_SIZES.values())[:i]) for i in range(len(IN_SIZES))]))

LANES = 128
VMEM_LIMIT_BYTES = 56 * 1024 * 1024

F32 = jnp.float32
BF16 = jnp.bfloat16
LOG2E = math.log2(math.e)


def _mm(a, b):
    return jnp.dot(a, b, preferred_element_type=F32)


def _mm_nt(a, b):
    return lax.dot_general(a, b, (((1,), (1,)), ((), ())), preferred_element_type=F32)


def _mm_tn(a, b):
    return lax.dot_general(a, b, (((0,), (0,)), ((), ())), preferred_element_type=F32)


def _sigmoid(x):
    return 1.0 / (1.0 + jnp.exp(-x))


def _params(*semantics):
    return pltpu.CompilerParams(dimension_semantics=semantics, vmem_limit_bytes=VMEM_LIMIT_BYTES)


def _ada_kernel(c_ref, w_ref, b_ref, o_ref):
    c = c_ref[...]
    ca = (c * _sigmoid(c)).astype(BF16)
    o_ref[...] = _mm(ca, w_ref[...].astype(BF16)) + b_ref[...]


def _ada(c, w, b):
    bsz, d = c.shape
    n = w.shape[1]
    return pl.pallas_call(
        _ada_kernel,
        grid=(n // d,),
        in_specs=[pl.BlockSpec((bsz, d), lambda j: (0, 0)),
                  pl.BlockSpec((d, d), lambda j: (0, j)),
                  pl.BlockSpec((1, d), lambda j: (0, j))],
        out_specs=pl.BlockSpec((bsz, d), lambda j: (0, j)),
        out_shape=jax.ShapeDtypeStruct((bsz, n), F32),
        compiler_params=_params("arbitrary"),
        name="ada_mod",
    )(c, w, b.reshape(1, n))


def _rope(x, cos, sin, lo):
    partner = jnp.where(lo, pltpu.roll(x, LANES - 16, 1), pltpu.roll(x, 16, 1))
    return x * cos + partner * sin


def _head_rms(x, ones_blk):
    ms = _mm((x * x).astype(BF16), ones_blk) * (1.0 / HEAD_DIM)
    return x * lax.rsqrt(ms + EPS)


def _inproj_kernel(x_ref, mul_ref, shift_ref, w_ref, cos_ref, sin_ref, ones_ref, qg_ref, kg_ref,
                   qa_ref, ka_ref, va_ref, za_ref, qr_ref, kr_ref, vr_ref, zr_ref, gl_ref, *, tm):
    x = x_ref[0]
    ms = jnp.mean(x * x, axis=-1, keepdims=True)
    h = (x * lax.rsqrt(ms + EPS) * mul_ref[0] + shift_ref[0]).astype(BF16)

    cos = cos_ref[...]
    sin = sin_ref[...]
    lane = lax.broadcasted_iota(jnp.int32, (tm, LANES), 1)
    lo = (lane & 31) < 16
    ones_blk = ones_ref[...]

    def rope_cols(y, scale_row):
        outs = []
        for j in range(y.shape[1] // LANES):
            blk = y[:, j * LANES:(j + 1) * LANES]
            if scale_row is not None:
                blk = blk * scale_row[:, j * LANES:(j + 1) * LANES]
            outs.append(_rope(blk, cos, sin, lo))
        return outs

    def seg(name, part=0, width=None):
        start = IN_OFFSETS[name] + part
        width = IN_SIZES[name] if width is None else width
        return _mm(h, w_ref[:, start:start + width])

    def store_heads(ref, blks):
        for j, blk in enumerate(blks):
            blk = blk.astype(BF16)
            ref[0, 2 * j] = blk[:, :HEAD_DIM]
            ref[0, 2 * j + 1] = blk[:, HEAD_DIM:]

    def gate_logits(j):
        gl_ref[0, :, j * 512:(j + 1) * 512] = _sigmoid(seg("gl", j * 512, 512)).astype(BF16)

    q_raw = seg("qa")
    kv = seg("ka", 0, IN_SIZES["ka"] + IN_SIZES["va"])
    gate_logits(0)
    gate_logits(1)
    store_heads(qa_ref, rope_cols(_head_rms(q_raw, ones_blk), qg_ref[...]))
    k = _head_rms(kv[:, :LANES], ones_blk[:LANES, :LANES])
    store_heads(ka_ref, rope_cols(k, kg_ref[...]))
    gate_logits(2)
    gate_logits(3)
    z = seg("za")
    za_ref[0] = (z * _sigmoid(z)).astype(BF16)
    z = seg("zr")
    zr_ref[0] = (z * _sigmoid(z)).astype(BF16)
    vt = kv[:, LANES:].T
    sub = lax.broadcasted_iota(jnp.int32, (VT_ROWS - HEAD_DIM, tm), 0)
    ones_rows = jnp.where(sub == 0, 1.0, 0.0)
    for g in range(ATTN_KV_HEADS):
        ext = jnp.concatenate([vt[g * HEAD_DIM:(g + 1) * HEAD_DIM], ones_rows], axis=0)
        va_ref[0, g] = ext.astype(BF16)
    store_heads(qr_ref, rope_cols(seg("qr"), None))
    store_heads(kr_ref, [blk * HEAD_DIM ** -0.5 for blk in rope_cols(seg("kr"), None)])
    vr_ref[0] = seg("vr").astype(BF16)


def _inproj(x, mul, shift, w_bf16, cos, sin, ones_blk, qg, kg, *, tm):
    bsz, s, d = x.shape
    n = w_bf16.shape[1]
    nt = s // tm
    const = lambda b, i: (0, 0)
    tok3 = lambda b, i: (b, i, 0)
    head4 = lambda b, i: (b, 0, i, 0)
    out_shape = [
        jax.ShapeDtypeStruct((bsz, ATTN_HEADS, s, HEAD_DIM), BF16),
        jax.ShapeDtypeStruct((bsz, ATTN_KV_HEADS, s, HEAD_DIM), BF16),
        jax.ShapeDtypeStruct((bsz, ATTN_KV_HEADS, VT_ROWS, s), BF16),
        jax.ShapeDtypeStruct((bsz, s, 512), BF16),
        jax.ShapeDtypeStruct((bsz, RET_HEADS, s, HEAD_DIM), BF16),
        jax.ShapeDtypeStruct((bsz, RET_HEADS, s, HEAD_DIM), BF16),
        jax.ShapeDtypeStruct((bsz, s, 512), BF16),
        jax.ShapeDtypeStruct((bsz, s, 512), BF16),
        jax.ShapeDtypeStruct((bsz, s, 2048), BF16),
    ]
    out_specs = [
        pl.BlockSpec((1, ATTN_HEADS, tm, HEAD_DIM), head4),
        pl.BlockSpec((1, ATTN_KV_HEADS, tm, HEAD_DIM), head4),
        pl.BlockSpec((1, ATTN_KV_HEADS, VT_ROWS, tm), lambda b, i: (b, 0, 0, i)),
        pl.BlockSpec((1, tm, 512), tok3),
        pl.BlockSpec((1, RET_HEADS, tm, HEAD_DIM), head4),
        pl.BlockSpec((1, RET_HEADS, tm, HEAD_DIM), head4),
        pl.BlockSpec((1, tm, 512), tok3),
        pl.BlockSpec((1, tm, 512), tok3),
        pl.BlockSpec((1, tm, 2048), tok3),
    ]
    return pl.pallas_call(
        functools.partial(_inproj_kernel, tm=tm),
        grid=(bsz, nt),
        in_specs=[pl.BlockSpec((1, tm, d), tok3),
                  pl.BlockSpec((1, 1, d), lambda b, i: (b, 0, 0)),
                  pl.BlockSpec((1, 1, d), lambda b, i: (b, 0, 0)),
                  pl.BlockSpec((d, n), const, pipeline_mode=pl.Buffered(1)),
                  pl.BlockSpec((tm, LANES), lambda b, i: (i, 0)),
                  pl.BlockSpec((tm, LANES), lambda b, i: (i, 0)),
                  pl.BlockSpec((512, 512), const),
                  pl.BlockSpec((1, 512), const),
                  pl.BlockSpec((1, LANES), const)],
        out_specs=out_specs,
        out_shape=out_shape,
        compiler_params=_params("arbitrary", "arbitrary"),
        name="in_proj",
    )(x, mul, shift, w_bf16, cos, sin, ones_blk, qg, kg)


def _attn_kernel(q_ref, k_ref, v_ref, z_ref, o_ref, *, tq, tk, cw, seq, lookahead):
    group = ATTN_HEADS // ATTN_KV_HEADS
    cols = group * tq
    q = q_ref[0].reshape(cols, HEAD_DIM)

    units = [(c, j) for j in range(seq // tk) for c in range(cols // cw)]

    def scores(c, j):
        return _mm_nt(k_ref[0, 0, j * tk:(j + 1) * tk, :], q[c * cw:(c + 1) * cw])

    m = [jnp.full((1, cw), -jnp.inf, F32)] * (cols // cw)
    acc = [jnp.zeros((VT_ROWS, cw), F32)] * (cols // cw)
    pending = [scores(*u) for u in units[:lookahead]]
    for i, (c, j) in enumerate(units):
        if i + lookahead < len(units):
            pending.append(scores(*units[i + lookahead]))
        st = pending.pop(0)
        m_new = jnp.maximum(m[c], jnp.max(st, axis=0, keepdims=True))
        pt = jnp.exp2(st - m_new).astype(BF16)
        acc[c] = jnp.exp2(m[c] - m_new) * acc[c] + _mm(v_ref[0, 0, :, j * tk:(j + 1) * tk], pt)
        m[c] = m_new
    acc = jnp.concatenate(acc, axis=1)
    out = acc[:HEAD_DIM] / acc[HEAD_DIM:HEAD_DIM + 1]
    out = jnp.concatenate([out[:, h * tq:(h + 1) * tq].T for h in range(group)], axis=1)
    o_ref[0] = (out * z_ref[0].astype(F32)).astype(BF16)


def _attention(qa, ka, va, za, *, tq, tk, cw, lookahead):
    bsz, _, s, _ = qa.shape
    group = ATTN_HEADS // ATTN_KV_HEADS
    width = group * HEAD_DIM
    return pl.pallas_call(
        functools.partial(_attn_kernel, tq=tq, tk=tk, cw=cw, seq=s, lookahead=lookahead),
        grid=(bsz, ATTN_KV_HEADS, s // tq),
        in_specs=[pl.BlockSpec((1, group, tq, HEAD_DIM), lambda b, g, i: (b, g, i, 0)),
                  pl.BlockSpec((1, 1, s, HEAD_DIM), lambda b, g, i: (b, g, 0, 0)),
                  pl.BlockSpec((1, 1, VT_ROWS, s), lambda b, g, i: (b, g, 0, 0)),
                  pl.BlockSpec((1, tq, width), lambda b, g, i: (b, i, g))],
        out_specs=pl.BlockSpec((1, tq, width), lambda b, g, i: (b, i, g)),
        out_shape=jax.ShapeDtypeStruct((bsz, s, ATTN_HEADS * HEAD_DIM), BF16),
        compiler_params=_params("arbitrary", "arbitrary", "arbitrary"),
        name="gqa_attention",
    )(qa, ka, va, za)


def _log_sigmoid(w):
    return -(jnp.maximum(-w, 0.0) + jnp.log1p(jnp.exp(-jnp.abs(w))))


def _ret_kernel(q_ref, k_ref, v_ref, z_ref, wf_ref, wb_ref, gn_ref, o_ref, *, chunk, seq, lookahead):
    nchunks = seq // chunk
    lg_f = _log_sigmoid(wf_ref[0])[0:1, 0:1]
    lg_b = _log_sigmoid(wb_ref[0])[0:1, 0:1]

    row = lax.broadcasted_iota(jnp.int32, (chunk, chunk), 0).astype(F32)
    colm = lax.broadcasted_iota(jnp.int32, (chunk, chunk), 1).astype(F32)
    diff = row - colm
    decay = jnp.where(diff >= 0, jnp.exp(lg_f * jnp.maximum(diff, 0.0)), jnp.exp(lg_b * jnp.maximum(-diff, 0.0)))
    idx = lax.broadcasted_iota(jnp.int32, (chunk, HEAD_DIM), 0).astype(F32)
    q_dec_f = jnp.exp(lg_f * (idx + 1.0))
    q_dec_b = jnp.exp(lg_b * (chunk - idx))
    k_dec_f = jnp.exp(lg_f * (chunk - 1.0 - idx))
    k_dec_b = jnp.exp(lg_b * idx)
    chunk_dec_f = jnp.exp(lg_f * chunk)
    chunk_dec_b = jnp.exp(lg_b * chunk)

    def rows(n):
        return slice(n * chunk, (n + 1) * chunk)

    def kv_update(n, k_dec):
        kd = (k_ref[0, 0, rows(n), :].astype(F32) * k_dec).astype(BF16)
        return _mm_tn(kd, v_ref[0, rows(n), :])

    def scan(order, k_dec, chunk_dec):
        updates = {n: kv_update(n, k_dec) for n in order}
        states, state = {}, jnp.zeros((HEAD_DIM, RET_VALUE_DIM), F32)
        for n in order:
            states[n] = state.astype(BF16)
            state = chunk_dec * state + updates[n]
        return states

    state_b = scan(range(nchunks - 1, -1, -1), k_dec_b, chunk_dec_b)
    state_f = scan(range(nchunks), k_dec_f, chunk_dec_f)

    def scores(n):
        return _mm_nt(q_ref[0, 0, rows(n), :], k_ref[0, 0, rows(n), :])

    gn = gn_ref[...]
    pending = [scores(n) for n in range(min(lookahead, nchunks))]
    for n in range(nchunks):
        if n + lookahead < nchunks:
            pending.append(scores(n + lookahead))
        qf = q_ref[0, 0, rows(n), :].astype(F32)
        o = _mm((pending.pop(0) * decay).astype(BF16), v_ref[0, rows(n), :])
        o += _mm((qf * q_dec_f).astype(BF16), state_f[n])
        o += _mm((qf * q_dec_b).astype(BF16), state_b[n])
        mu = jnp.mean(o, axis=-1, keepdims=True)
        cen = o - mu
        var = jnp.mean(cen * cen, axis=-1, keepdims=True)
        y = cen * lax.rsqrt(var + EPS) * gn * z_ref[0, rows(n), :].astype(F32)
        o_ref[0, rows(n), :] = y.astype(BF16)


def _retention(qr, kr, vr, zr, wf, wb, gn, *, chunk, lookahead):
    bsz, heads, s, _ = qr.shape
    head_blk = lambda b, h: (b, h, 0, 0)
    col_blk = lambda b, h: (b, 0, h)
    return pl.pallas_call(
        functools.partial(_ret_kernel, chunk=chunk, seq=s, lookahead=lookahead),
        grid=(bsz, heads),
        in_specs=[pl.BlockSpec((1, 1, s, HEAD_DIM), head_blk),
                  pl.BlockSpec((1, 1, s, HEAD_DIM), head_blk),
                  pl.BlockSpec((1, s, RET_VALUE_DIM), col_blk),
                  pl.BlockSpec((1, s, RET_VALUE_DIM), col_blk),
                  pl.BlockSpec((1, 8, LANES), lambda b, h: (h, 0, 0)),
                  pl.BlockSpec((1, 8, LANES), lambda b, h: (h, 0, 0)),
                  pl.BlockSpec((1, RET_VALUE_DIM), lambda b, h: (0, h))],
        out_specs=pl.BlockSpec((1, s, RET_VALUE_DIM), col_blk),
        out_shape=jax.ShapeDtypeStruct((bsz, s, heads * RET_VALUE_DIM), BF16),
        compiler_params=_params("arbitrary", "arbitrary"),
        name="retention",
    )(qr, kr, vr, zr, wf, wb, gn)


def _outproj_kernel(ya_ref, yr_ref, gl_ref, x_ref, gate_ref, wpa_ref, wpr_ref, wout_ref, gpost_ref, o_ref, *, sub):
    d = x_ref.shape[-1]
    tiles = [pl.ds(r, sub) for r in range(0, x_ref.shape[1], sub)]
    branch = [(_mm(ya_ref[0, rows, :], wpa_ref[...]), _mm(yr_ref[0, rows, :], wpr_ref[...])) for rows in tiles]
    for rows, (pa, pr) in zip(tiles, branch):
        merged = gl_ref[0, rows, :d].astype(F32) * pa + gl_ref[0, rows, d:].astype(F32) * pr
        z = _mm(merged.astype(BF16), wout_ref[...])
        ms = jnp.mean(z * z, axis=-1, keepdims=True)
        y = z * lax.rsqrt(ms + EPS) * gpost_ref[...]
        o_ref[0, rows, :] = x_ref[0, rows, :] + gate_ref[0] * y


def _outproj(ya, yr, gl, x, gate, wpa, wpr, wout, gpost, *, tm, sub):
    bsz, s, d = x.shape
    tok3 = lambda b, i: (b, i, 0)
    const = lambda b, i: (0, 0)
    return pl.pallas_call(
        functools.partial(_outproj_kernel, sub=sub),
        grid=(bsz, s // tm),
        in_specs=[pl.BlockSpec((1, tm, ya.shape[-1]), tok3),
                  pl.BlockSpec((1, tm, yr.shape[-1]), tok3),
                  pl.BlockSpec((1, tm, 2 * d), tok3),
                  pl.BlockSpec((1, tm, d), tok3),
                  pl.BlockSpec((1, 1, d), lambda b, i: (b, 0, 0)),
                  pl.BlockSpec(wpa.shape, const),
                  pl.BlockSpec(wpr.shape, const),
                  pl.BlockSpec(wout.shape, const),
                  pl.BlockSpec((1, d), const)],
        out_specs=pl.BlockSpec((1, tm, d), tok3),
        out_shape=jax.ShapeDtypeStruct((bsz, s, d), F32),
        compiler_params=_params("arbitrary", "arbitrary"),
        name="out_proj",
    )(ya, yr, gl, x, gate, wpa, wpr, wout, gpost)


def _rope_tables(seq):
    t = jnp.arange(seq)
    row = (t // GRID_W).astype(F32)
    colp = (t % GRID_W).astype(F32)
    half = HEAD_DIM // 2
    inv_freq = ROPE_THETA ** (-jnp.arange(0, half, 2, dtype=F32) / half)
    ang_r = row[:, None] * inv_freq[None, :]
    ang_c = colp[:, None] * inv_freq[None, :]
    cos = jnp.concatenate([jnp.cos(ang_r)] * 2 + [jnp.cos(ang_c)] * 2, axis=-1)
    sin = jnp.concatenate([-jnp.sin(ang_r), jnp.sin(ang_r), -jnp.sin(ang_c), jnp.sin(ang_c)], axis=-1)
    return jnp.tile(cos, (1, 2)), jnp.tile(sin, (1, 2))


def _layer(x, mod, g_pre, w_in, qn_g, kn_g, w_dec_f, w_dec_b, gn_g, w_pa, w_pr, w_out, g_post, tables):
    bsz, s, d = x.shape
    shift, scale, gate = jnp.split(mod, 3, axis=-1)
    mul = (g_pre[None, :] * (1.0 + scale)).reshape(bsz, 1, d)
    shift = shift.reshape(bsz, 1, d)
    gate = gate.reshape(bsz, 1, d)

    blk = jnp.arange(512) // HEAD_DIM
    ones_blk = (blk[:, None] == blk[None, :]).astype(BF16)
    qg = jnp.tile(qn_g * (HEAD_DIM ** -0.5 * LOG2E), ATTN_HEADS).reshape(1, 512)
    kg = jnp.tile(kn_g, ATTN_KV_HEADS).reshape(1, LANES)
    cos, sin = tables

    qa, ka, va, za, qr, kr, vr, zr, gl = _inproj(x, mul, shift, w_in.astype(BF16), cos, sin, ones_blk, qg, kg, tm=1024)
    ya = _attention(qa, ka, va, za, tq=512, tk=256, cw=256, lookahead=12)
    wf = jnp.broadcast_to(w_dec_f.astype(F32)[:, None, None], (RET_HEADS, 8, LANES))
    wb = jnp.broadcast_to(w_dec_b.astype(F32)[:, None, None], (RET_HEADS, 8, LANES))
    yr = _retention(qr, kr, vr, zr, wf, wb, gn_g.reshape(1, -1), chunk=256, lookahead=3)
    return _outproj(ya, yr, gl, x, gate, w_pa.astype(BF16), w_pr.astype(BF16), w_out.astype(BF16),
                    g_post.reshape(1, d), tm=1024, sub=512)


def kernel(x, c, w_ada, b_ada, g_pre, w_in, qn_g, kn_g, w_dec_f, w_dec_b, gn_g, w_pa, w_pr, w_out, g_post):
    tables = _rope_tables(x.shape[1])
    for l in range(w_ada.shape[0]):
        mod = _ada(c, w_ada[l], b_ada[l])
        x = _layer(x, mod, g_pre[l], w_in[l], qn_g[l], kn_g[l], w_dec_f[l], w_dec_b[l], gn_g[l],
                   w_pa[l], w_pr[l], w_out[l], g_post[l], tables)
    return x
```

```python
import functools
import math

import jax
import jax.numpy as jnp
from jax import lax
from jax.experimental import pallas as pl
from jax.experimental.pallas import tpu as pltpu

GRID_W = 64
ATTN_HEADS = 8
ATTN_KV_HEADS = 2
HEAD_DIM = 64
RET_HEADS = 4
RET_VALUE_DIM = 128
VT_ROWS = 80
ROPE_THETA = 10000.0
EPS = 1e-6

IN_SIZES = {"qa": 512, "ka": 128, "va": 128, "za": 512, "qr": 256, "kr": 256, "vr": 512, "zr": 512, "gl": 2048}
IN_OFFSETS = dict(zip(IN_SIZES, [sum(list(IN_SIZES.values())[:i]) for i in range(len(IN_SIZES))]))

LANES = 128
VMEM_LIMIT_BYTES = 56 * 1024 * 1024

F32 = jnp.float32
BF16 = jnp.bfloat16
LOG2E = math.log2(math.e)


def _mm(a, b):
    return jnp.dot(a, b, preferred_element_type=F32)


def _mm_nt(a, b):
    return lax.dot_general(a, b, (((1,), (1,)), ((), ())), preferred_element_type=F32)


def _mm_tn(a, b):
    return lax.dot_general(a, b, (((0,), (0,)), ((), ())), preferred_element_type=F32)


def _sigmoid(x):
    return 1.0 / (1.0 + jnp.exp(-x))


def _params(*semantics):
    return pltpu.CompilerParams(dimension_semantics=semantics, vmem_limit_bytes=VMEM_LIMIT_BYTES)


def _ada_kernel(c_ref, w_ref, b_ref, o_ref):
    c = c_ref[...]
    ca = (c * _sigmoid(c)).astype(BF16)
    o_ref[...] = _mm(ca, w_ref[...].astype(BF16)) + b_ref[...]


def _ada(c, w, b):
    bsz, d = c.shape
    n = w.shape[1]
    return pl.pallas_call(
        _ada_kernel,
        grid=(n // d,),
        in_specs=[pl.BlockSpec((bsz, d), lambda j: (0, 0)),
                  pl.BlockSpec((d, d), lambda j: (0, j)),
                  pl.BlockSpec((1, d), lambda j: (0, j))],
        out_specs=pl.BlockSpec((bsz, d), lambda j: (0, j)),
        out_shape=jax.ShapeDtypeStruct((bsz, n), F32),
        compiler_params=_params("arbitrary"),
        name="ada_mod",
    )(c, w, b.reshape(1, n))


def _rope(x, cos, sin, lo):
    partner = jnp.where(lo, pltpu.roll(x, LANES - 16, 1), pltpu.roll(x, 16, 1))
    return x * cos + partner * sin


def _head_rms(x, ones_blk):
    ms = _mm((x * x).astype(BF16), ones_blk) * (1.0 / HEAD_DIM)
    return x * lax.rsqrt(ms + EPS)


def _inproj_kernel(x_ref, mul_ref, shift_ref, w_ref, cos_ref, sin_ref, ones_ref, qg_ref, kg_ref,
                   qa_ref, ka_ref, va_ref, za_ref, qr_ref, kr_ref, vr_ref, zr_ref, gl_ref, *, tm):
    x = x_ref[0]
    ms = jnp.mean(x * x, axis=-1, keepdims=True)
    h = (x * lax.rsqrt(ms + EPS) * mul_ref[0] + shift_ref[0]).astype(BF16)

    cos = cos_ref[...]
    sin = sin_ref[...]
    lane = lax.broadcasted_iota(jnp.int32, (tm, LANES), 1)
    lo = (lane & 31) < 16
    ones_blk = ones_ref[...]

    def rope_cols(y, scale_row):
        outs = []
        for j in range(y.shape[1] // LANES):
            blk = y[:, j * LANES:(j + 1) * LANES]
            if scale_row is not None:
                blk = blk * scale_row[:, j * LANES:(j + 1) * LANES]
            outs.append(_rope(blk, cos, sin, lo))
        return outs

    def seg(name, part=0, width=None):
        start = IN_OFFSETS[name] + part
        width = IN_SIZES[name] if width is None else width
        return _mm(h, w_ref[:, start:start + width])

    def store_heads(ref, blks):
        for j, blk in enumerate(blks):
            blk = blk.astype(BF16)
            ref[0, 2 * j] = blk[:, :HEAD_DIM]
            ref[0, 2 * j + 1] = blk[:, HEAD_DIM:]

    for j in range(IN_SIZES["gl"] // 512):
        gl_ref[0, :, j * 512:(j + 1) * 512] = _sigmoid(seg("gl", j * 512, 512)).astype(BF16)
    z = seg("za")
    za_ref[0] = (z * _sigmoid(z)).astype(BF16)
    z = seg("zr")
    zr_ref[0] = (z * _sigmoid(z)).astype(BF16)
    q = _head_rms(seg("qa"), ones_blk)
    store_heads(qa_ref, rope_cols(q, qg_ref[...]))
    kv = seg("ka", 0, IN_SIZES["ka"] + IN_SIZES["va"])
    k = _head_rms(kv[:, :LANES], ones_blk[:LANES, :LANES])
    store_heads(ka_ref, rope_cols(k, kg_ref[...]))
    vt = kv[:, LANES:].T
    sub = lax.broadcasted_iota(jnp.int32, (VT_ROWS - HEAD_DIM, tm), 0)
    ones_rows = jnp.where(sub == 0, 1.0, 0.0)
    for g in range(ATTN_KV_HEADS):
        ext = jnp.concatenate([vt[g * HEAD_DIM:(g + 1) * HEAD_DIM], ones_rows], axis=0)
        va_ref[0, g] = ext.astype(BF16)
    store_heads(qr_ref, rope_cols(seg("qr"), None))
    store_heads(kr_ref, [blk * HEAD_DIM ** -0.5 for blk in rope_cols(seg("kr"), None)])
    vr_ref[0] = seg("vr").astype(BF16)


def _inproj(x, mul, shift, w_bf16, cos, sin, ones_blk, qg, kg, *, tm):
    bsz, s, d = x.shape
    n = w_bf16.shape[1]
    nt = s // tm
    const = lambda b, i: (0, 0)
    tok3 = lambda b, i: (b, i, 0)
    head4 = lambda b, i: (b, 0, i, 0)
    out_shape = [
        jax.ShapeDtypeStruct((bsz, ATTN_HEADS, s, HEAD_DIM), BF16),
        jax.ShapeDtypeStruct((bsz, ATTN_KV_HEADS, s, HEAD_DIM), BF16),
        jax.ShapeDtypeStruct((bsz, ATTN_KV_HEADS, VT_ROWS, s), BF16),
        jax.ShapeDtypeStruct((bsz, s, 512), BF16),
        jax.ShapeDtypeStruct((bsz, RET_HEADS, s, HEAD_DIM), BF16),
        jax.ShapeDtypeStruct((bsz, RET_HEADS, s, HEAD_DIM), BF16),
        jax.ShapeDtypeStruct((bsz, s, 512), BF16),
        jax.ShapeDtypeStruct((bsz, s, 512), BF16),
        jax.ShapeDtypeStruct((bsz, s, 2048), BF16),
    ]
    out_specs = [
        pl.BlockSpec((1, ATTN_HEADS, tm, HEAD_DIM), head4),
        pl.BlockSpec((1, ATTN_KV_HEADS, tm, HEAD_DIM), head4),
        pl.BlockSpec((1, ATTN_KV_HEADS, VT_ROWS, tm), lambda b, i: (b, 0, 0, i)),
        pl.BlockSpec((1, tm, 512), tok3),
        pl.BlockSpec((1, RET_HEADS, tm, HEAD_DIM), head4),
        pl.BlockSpec((1, RET_HEADS, tm, HEAD_DIM), head4),
        pl.BlockSpec((1, tm, 512), tok3),
        pl.BlockSpec((1, tm, 512), tok3),
        pl.BlockSpec((1, tm, 2048), tok3),
    ]
    return pl.pallas_call(
        functools.partial(_inproj_kernel, tm=tm),
        grid=(bsz, nt),
        in_specs=[pl.BlockSpec((1, tm, d), tok3),
                  pl.BlockSpec((1, 1, d), lambda b, i: (b, 0, 0)),
                  pl.BlockSpec((1, 1, d), lambda b, i: (b, 0, 0)),
                  pl.BlockSpec((d, n), const, pipeline_mode=pl.Buffered(1)),
                  pl.BlockSpec((tm, LANES), lambda b, i: (i, 0)),
                  pl.BlockSpec((tm, LANES), lambda b, i: (i, 0)),
                  pl.BlockSpec((512, 512), const),
                  pl.BlockSpec((1, 512), const),
                  pl.BlockSpec((1, LANES), const)],
        out_specs=out_specs,
        out_shape=out_shape,
        compiler_params=_params("arbitrary", "arbitrary"),
        name="in_proj",
    )(x, mul, shift, w_bf16, cos, sin, ones_blk, qg, kg)


def _attn_kernel(q_ref, k_ref, v_ref, z_ref, o_ref, *, tq, tk, cw, seq, lookahead):
    group = ATTN_HEADS // ATTN_KV_HEADS
    cols = group * tq
    q = q_ref[0].reshape(cols, HEAD_DIM)

    units = [(c, j) for j in range(seq // tk) for c in range(cols // cw)]

    def scores(c, j):
        return _mm_nt(k_ref[0, 0, j * tk:(j + 1) * tk, :], q[c * cw:(c + 1) * cw])

    m = [jnp.full((1, cw), -jnp.inf, F32)] * (cols // cw)
    acc = [jnp.zeros((VT_ROWS, cw), F32)] * (cols // cw)
    pending = [scores(*u) for u in units[:lookahead]]
    for i, (c, j) in enumerate(units):
        if i + lookahead < len(units):
            pending.append(scores(*units[i + lookahead]))
        st = pending.pop(0)
        m_new = jnp.maximum(m[c], jnp.max(st, axis=0, keepdims=True))
        pt = jnp.exp2(st - m_new).astype(BF16)
        acc[c] = jnp.exp2(m[c] - m_new) * acc[c] + _mm(v_ref[0, 0, :, j * tk:(j + 1) * tk], pt)
        m[c] = m_new
    acc = jnp.concatenate(acc, axis=1)
    out = acc[:HEAD_DIM] / acc[HEAD_DIM:HEAD_DIM + 1]
    out = jnp.concatenate([out[:, h * tq:(h + 1) * tq].T for h in range(group)], axis=1)
    o_ref[0] = (out * z_ref[0].astype(F32)).astype(BF16)


def _attention(qa, ka, va, za, *, tq, tk, cw, lookahead):
    bsz, _, s, _ = qa.shape
    group = ATTN_HEADS // ATTN_KV_HEADS
    width = group * HEAD_DIM
    return pl.pallas_call(
        functools.partial(_attn_kernel, tq=tq, tk=tk, cw=cw, seq=s, lookahead=lookahead),
        grid=(bsz, ATTN_KV_HEADS, s // tq),
        in_specs=[pl.BlockSpec((1, group, tq, HEAD_DIM), lambda b, g, i: (b, g, i, 0)),
                  pl.BlockSpec((1, 1, s, HEAD_DIM), lambda b, g, i: (b, g, 0, 0)),
                  pl.BlockSpec((1, 1, VT_ROWS, s), lambda b, g, i: (b, g, 0, 0)),
                  pl.BlockSpec((1, tq, width), lambda b, g, i: (b, i, g))],
        out_specs=pl.BlockSpec((1, tq, width), lambda b, g, i: (b, i, g)),
        out_shape=jax.ShapeDtypeStruct((bsz, s, ATTN_HEADS * HEAD_DIM), BF16),
        compiler_params=_params("arbitrary", "arbitrary", "arbitrary"),
        name="gqa_attention",
    )(qa, ka, va, za)


def _log_sigmoid(w):
    return -(jnp.maximum(-w, 0.0) + jnp.log1p(jnp.exp(-jnp.abs(w))))


def _ret_kernel(q_ref, k_ref, v_ref, z_ref, wf_ref, wb_ref, gn_ref, o_ref, *, chunk, seq, lookahead):
    nchunks = seq // chunk
    lg_f = _log_sigmoid(wf_ref[0])[0:1, 0:1]
    lg_b = _log_sigmoid(wb_ref[0])[0:1, 0:1]

    row = lax.broadcasted_iota(jnp.int32, (chunk, chunk), 0).astype(F32)
    colm = lax.broadcasted_iota(jnp.int32, (chunk, chunk), 1).astype(F32)
    diff = row - colm
    decay = jnp.where(diff >= 0, jnp.exp(lg_f * jnp.maximum(diff, 0.0)), jnp.exp(lg_b * jnp.maximum(-diff, 0.0)))
    idx = lax.broadcasted_iota(jnp.int32, (chunk, HEAD_DIM), 0).astype(F32)
    q_dec_f = jnp.exp(lg_f * (idx + 1.0))
    q_dec_b = jnp.exp(lg_b * (chunk - idx))
    k_dec_f = jnp.exp(lg_f * (chunk - 1.0 - idx))
    k_dec_b = jnp.exp(lg_b * idx)
    chunk_dec_f = jnp.exp(lg_f * chunk)
    chunk_dec_b = jnp.exp(lg_b * chunk)

    def rows(n):
        return slice(n * chunk, (n + 1) * chunk)

    def kv_update(n, k_dec):
        kd = (k_ref[0, 0, rows(n), :].astype(F32) * k_dec).astype(BF16)
        return _mm_tn(kd, v_ref[0, rows(n), :])

    def scan(order, k_dec, chunk_dec):
        updates = {n: kv_update(n, k_dec) for n in order}
        states, state = {}, jnp.zeros((HEAD_DIM, RET_VALUE_DIM), F32)
        for n in order:
            states[n] = state.astype(BF16)
            state = chunk_dec * state + updates[n]
        return states

    state_b = scan(range(nchunks - 1, -1, -1), k_dec_b, chunk_dec_b)
    state_f = scan(range(nchunks), k_dec_f, chunk_dec_f)

    def scores(n):
        return _mm_nt(q_ref[0, 0, rows(n), :], k_ref[0, 0, rows(n), :])

    gn = gn_ref[...]
    pending = [scores(n) for n in range(min(lookahead, nchunks))]
    for n in range(nchunks):
        if n + lookahead < nchunks:
            pending.append(scores(n + lookahead))
        qf = q_ref[0, 0, rows(n), :].astype(F32)
        o = _mm((pending.pop(0) * decay).astype(BF16), v_ref[0, rows(n), :])
        o += _mm((qf * q_dec_f).astype(BF16), state_f[n])
        o += _mm((qf * q_dec_b).astype(BF16), state_b[n])
        mu = jnp.mean(o, axis=-1, keepdims=True)
        cen = o - mu
        var = jnp.mean(cen * cen, axis=-1, keepdims=True)
        y = cen * lax.rsqrt(var + EPS) * gn * z_ref[0, rows(n), :].astype(F32)
        o_ref[0, rows(n), :] = y.astype(BF16)


def _retention(qr, kr, vr, zr, wf, wb, gn, *, chunk, lookahead):
    bsz, heads, s, _ = qr.shape
    head_blk = lambda b, h: (b, h, 0, 0)
    col_blk = lambda b, h: (b, 0, h)
    return pl.pallas_call(
        functools.partial(_ret_kernel, chunk=chunk, seq=s, lookahead=lookahead),
        grid=(bsz, heads),
        in_specs=[pl.BlockSpec((1, 1, s, HEAD_DIM), head_blk),
                  pl.BlockSpec((1, 1, s, HEAD_DIM), head_blk),
                  pl.BlockSpec((1, s, RET_VALUE_DIM), col_blk),
                  pl.BlockSpec((1, s, RET_VALUE_DIM), col_blk),
                  pl.BlockSpec((1, 8, LANES), lambda b, h: (h, 0, 0)),
                  pl.BlockSpec((1, 8, LANES), lambda b, h: (h, 0, 0)),
                  pl.BlockSpec((1, RET_VALUE_DIM), lambda b, h: (0, h))],
        out_specs=pl.BlockSpec((1, s, RET_VALUE_DIM), col_blk),
        out_shape=jax.ShapeDtypeStruct((bsz, s, heads * RET_VALUE_DIM), BF16),
        compiler_params=_params("arbitrary", "arbitrary"),
        name="retention",
    )(qr, kr, vr, zr, wf, wb, gn)


def _outproj_kernel(ya_ref, yr_ref, gl_ref, x_ref, gate_ref, wpa_ref, wpr_ref, wout_ref, gpost_ref, o_ref, *, sub):
    d = x_ref.shape[-1]
    tiles = [pl.ds(r, sub) for r in range(0, x_ref.shape[1], sub)]
    branch = [(_mm(ya_ref[0, rows, :], wpa_ref[...]), _mm(yr_ref[0, rows, :], wpr_ref[...])) for rows in tiles]
    for rows, (pa, pr) in zip(tiles, branch):
        merged = gl_ref[0, rows, :d].astype(F32) * pa + gl_ref[0, rows, d:].astype(F32) * pr
        z = _mm(merged.astype(BF16), wout_ref[...])
        ms = jnp.mean(z * z, axis=-1, keepdims=True)
        y = z * lax.rsqrt(ms + EPS) * gpost_ref[...]
        o_ref[0, rows, :] = x_ref[0, rows, :] + gate_ref[0] * y


def _outproj(ya, yr, gl, x, gate, wpa, wpr, wout, gpost, *, tm, sub):
    bsz, s, d = x.shape
    tok3 = lambda b, i: (b, i, 0)
    const = lambda b, i: (0, 0)
    return pl.pallas_call(
        functools.partial(_outproj_kernel, sub=sub),
        grid=(bsz, s // tm),
        in_specs=[pl.BlockSpec((1, tm, ya.shape[-1]), tok3),
                  pl.BlockSpec((1, tm, yr.shape[-1]), tok3),
                  pl.BlockSpec((1, tm, 2 * d), tok3),
                  pl.BlockSpec((1, tm, d), tok3),
                  pl.BlockSpec((1, 1, d), lambda b, i: (b, 0, 0)),
                  pl.BlockSpec(wpa.shape, const),
                  pl.BlockSpec(wpr.shape, const),
                  pl.BlockSpec(wout.shape, const),
                  pl.BlockSpec((1, d), const)],
        out_specs=pl.BlockSpec((1, tm, d), tok3),
        out_shape=jax.ShapeDtypeStruct((bsz, s, d), F32),
        compiler_params=_params("arbitrary", "arbitrary"),
        name="out_proj",
    )(ya, yr, gl, x, gate, wpa, wpr, wout, gpost)


def _rope_tables(seq):
    t = jnp.arange(seq)
    row = (t // GRID_W).astype(F32)
    colp = (t % GRID_W).astype(F32)
    half = HEAD_DIM // 2
    inv_freq = ROPE_THETA ** (-jnp.arange(0, half, 2, dtype=F32) / half)
    ang_r = row[:, None] * inv_freq[None, :]
    ang_c = colp[:, None] * inv_freq[None, :]
    cos = jnp.concatenate([jnp.cos(ang_r)] * 2 + [jnp.cos(ang_c)] * 2, axis=-1)
    sin = jnp.concatenate([-jnp.sin(ang_r), jnp.sin(ang_r), -jnp.sin(ang_c), jnp.sin(ang_c)], axis=-1)
    return jnp.tile(cos, (1, 2)), jnp.tile(sin, (1, 2))


def _layer(x, mod, g_pre, w_in, qn_g, kn_g, w_dec_f, w_dec_b, gn_g, w_pa, w_pr, w_out, g_post, tables):
    bsz, s, d = x.shape
    shift, scale, gate = jnp.split(mod, 3, axis=-1)
    mul = (g_pre[None, :] * (1.0 + scale)).reshape(bsz, 1, d)
    shift = shift.reshape(bsz, 1, d)
    gate = gate.reshape(bsz, 1, d)

    blk = jnp.arange(512) // HEAD_DIM
    ones_blk = (blk[:, None] == blk[None, :]).astype(BF16)
    qg = jnp.tile(qn_g * (HEAD_DIM ** -0.5 * LOG2E), ATTN_HEADS).reshape(1, 512)
    kg = jnp.tile(kn_g, ATTN_KV_HEADS).reshape(1, LANES)
    cos, sin = tables

    qa, ka, va, za, qr, kr, vr, zr, gl = _inproj(x, mul, shift, w_in.astype(BF16), cos, sin, ones_blk, qg, kg, tm=1024)
    ya = _attention(qa, ka, va, za, tq=512, tk=512, cw=256, lookahead=8)
    wf = jnp.broadcast_to(w_dec_f.astype(F32)[:, None, None], (RET_HEADS, 8, LANES))
    wb = jnp.broadcast_to(w_dec_b.astype(F32)[:, None, None], (RET_HEADS, 8, LANES))
    yr = _retention(qr, kr, vr, zr, wf, wb, gn_g.reshape(1, -1), chunk=256, lookahead=3)
    return _outproj(ya, yr, gl, x, gate, w_pa.astype(BF16), w_pr.astype(BF16), w_out.astype(BF16),
                    g_post.reshape(1, d), tm=1024, sub=512)


def kernel(x, c, w_ada, b_ada, g_pre, w_in, qn_g, kn_g, w_dec_f, w_dec_b, gn_g, w_pa, w_pr, w_out, g_post):
    tables = _rope_tables(x.shape[1])
    for l in range(w_ada.shape[0]):
        mod = _ada(c, w_ada[l], b_ada[l])
        x = _layer(x, mod, g_pre[l], w_in[l], qn_g[l], kn_g[l], w_dec_f[l], w_dec_b[l], gn_g[l],
                   w_pa[l], w_pr[l], w_out[l], g_post[l], tables)
    return x
```

```python
import functools
import math

import jax
import jax.numpy as jnp
from jax import lax
from jax.experimental import pallas as pl
from jax.experimental.pallas import tpu as pltpu

GRID_W = 64
ATTN_HEADS = 8
ATTN_KV_HEADS = 2
HEAD_DIM = 64
RET_HEADS = 4
RET_VALUE_DIM = 128
VT_ROWS = 80
ROPE_THETA = 10000.0
EPS = 1e-6

IN_SIZES = {"qa": 512, "ka": 128, "va": 128, "za": 512, "qr": 256, "kr": 256, "vr": 512, "zr": 512, "gl": 2048}
IN_OFFSETS = dict(zip(IN_SIZES, [sum(list(IN_SIZES.values())[:i]) for i in range(len(IN_SIZES))]))

LANES = 128
VMEM_LIMIT_BYTES = 56 * 1024 * 1024

F32 = jnp.float32
BF16 = jnp.bfloat16
LOG2E = math.log2(math.e)


def _mm(a, b):
    return jnp.dot(a, b, preferred_element_type=F32)


def _mm_nt(a, b):
    return lax.dot_general(a, b, (((1,), (1,)), ((), ())), preferred_element_type=F32)


def _mm_tn(a, b):
    return lax.dot_general(a, b, (((0,), (0,)), ((), ())), preferred_element_type=F32)


def _sigmoid(x):
    return 1.0 / (1.0 + jnp.exp(-x))


def _params(*semantics):
    return pltpu.CompilerParams(dimension_semantics=semantics, vmem_limit_bytes=VMEM_LIMIT_BYTES)


def _ada_kernel(c_ref, w_ref, b_ref, o_ref):
    c = c_ref[...]
    ca = (c * _sigmoid(c)).astype(BF16)
    o_ref[...] = _mm(ca, w_ref[...].astype(BF16)) + b_ref[...]


def _ada(c, w, b):
    bsz, d = c.shape
    n = w.shape[1]
    return pl.pallas_call(
        _ada_kernel,
        grid=(n // d,),
        in_specs=[pl.BlockSpec((bsz, d), lambda j: (0, 0)),
                  pl.BlockSpec((d, d), lambda j: (0, j)),
                  pl.BlockSpec((1, d), lambda j: (0, j))],
        out_specs=pl.BlockSpec((bsz, d), lambda j: (0, j)),
        out_shape=jax.ShapeDtypeStruct((bsz, n), F32),
        compiler_params=_params("arbitrary"),
        name="ada_mod",
    )(c, w, b.reshape(1, n))


def _rope(x, cos, sin, lo):
    partner = jnp.where(lo, pltpu.roll(x, LANES - 16, 1), pltpu.roll(x, 16, 1))
    return x * cos + partner * sin


def _head_rms(x, ones_blk):
    ms = _mm((x * x).astype(BF16), ones_blk) * (1.0 / HEAD_DIM)
    return x * lax.rsqrt(ms + EPS)


def _inproj_kernel(x_ref, mul_ref, shift_ref, w_ref, cos_ref, sin_ref, ones_ref, qg_ref, kg_ref,
                   qa_ref, ka_ref, va_ref, za_ref, qr_ref, kr_ref, vr_ref, zr_ref, gl_ref, *, tm):
    x = x_ref[0]
    ms = jnp.mean(x * x, axis=-1, keepdims=True)
    h = (x * lax.rsqrt(ms + EPS) * mul_ref[0] + shift_ref[0]).astype(BF16)

    cos = cos_ref[...]
    sin = sin_ref[...]
    lane = lax.broadcasted_iota(jnp.int32, (tm, LANES), 1)
    lo = (lane & 31) < 16
    ones_blk = ones_ref[...]

    def rope_cols(y, scale_row):
        outs = []
        for j in range(y.shape[1] // LANES):
            blk = y[:, j * LANES:(j + 1) * LANES]
            if scale_row is not None:
                blk = blk * scale_row[:, j * LANES:(j + 1) * LANES]
            outs.append(_rope(blk, cos, sin, lo))
        return outs

    def seg(name, part=0, width=None):
        start = IN_OFFSETS[name] + part
        width = IN_SIZES[name] if width is None else width
        return _mm(h, w_ref[:, start:start + width])

    def store_heads(ref, blks):
        for j, blk in enumerate(blks):
            blk = blk.astype(BF16)
            ref[0, 2 * j] = blk[:, :HEAD_DIM]
            ref[0, 2 * j + 1] = blk[:, HEAD_DIM:]

    for j in range(IN_SIZES["gl"] // 512):
        gl_ref[0, :, j * 512:(j + 1) * 512] = _sigmoid(seg("gl", j * 512, 512)).astype(BF16)
    z = seg("za")
    za_ref[0] = (z * _sigmoid(z)).astype(BF16)
    z = seg("zr")
    zr_ref[0] = (z * _sigmoid(z)).astype(BF16)
    q = _head_rms(seg("qa"), ones_blk)
    for j, blk in enumerate(rope_cols(q, qg_ref[...])):
        blk_t = blk.T.astype(BF16)
        qa_ref[0, 2 * j] = blk_t[:HEAD_DIM]
        qa_ref[0, 2 * j + 1] = blk_t[HEAD_DIM:]
    kv = seg("ka", 0, IN_SIZES["ka"] + IN_SIZES["va"])
    k = _head_rms(kv[:, :LANES], ones_blk[:LANES, :LANES])
    store_heads(ka_ref, rope_cols(k, kg_ref[...]))
    vt = kv[:, LANES:].T
    sub = lax.broadcasted_iota(jnp.int32, (VT_ROWS - HEAD_DIM, tm), 0)
    ones_rows = jnp.where(sub == 0, 1.0, 0.0)
    for g in range(ATTN_KV_HEADS):
        ext = jnp.concatenate([vt[g * HEAD_DIM:(g + 1) * HEAD_DIM], ones_rows], axis=0)
        va_ref[0, g] = ext.astype(BF16)
    store_heads(qr_ref, rope_cols(seg("qr"), None))
    store_heads(kr_ref, [blk * HEAD_DIM ** -0.5 for blk in rope_cols(seg("kr"), None)])
    vr_ref[0] = seg("vr").astype(BF16)


def _inproj(x, mul, shift, w_bf16, cos, sin, ones_blk, qg, kg, *, tm):
    bsz, s, d = x.shape
    n = w_bf16.shape[1]
    nt = s // tm
    const = lambda b, i: (0, 0)
    tok3 = lambda b, i: (b, i, 0)
    head4 = lambda b, i: (b, 0, i, 0)
    out_shape = [
        jax.ShapeDtypeStruct((bsz, ATTN_HEADS, HEAD_DIM, s), BF16),
        jax.ShapeDtypeStruct((bsz, ATTN_KV_HEADS, s, HEAD_DIM), BF16),
        jax.ShapeDtypeStruct((bsz, ATTN_KV_HEADS, VT_ROWS, s), BF16),
        jax.ShapeDtypeStruct((bsz, s, 512), BF16),
        jax.ShapeDtypeStruct((bsz, RET_HEADS, s, HEAD_DIM), BF16),
        jax.ShapeDtypeStruct((bsz, RET_HEADS, s, HEAD_DIM), BF16),
        jax.ShapeDtypeStruct((bsz, s, 512), BF16),
        jax.ShapeDtypeStruct((bsz, s, 512), BF16),
        jax.ShapeDtypeStruct((bsz, s, 2048), BF16),
    ]
    out_specs = [
        pl.BlockSpec((1, ATTN_HEADS, HEAD_DIM, tm), lambda b, i: (b, 0, 0, i)),
        pl.BlockSpec((1, ATTN_KV_HEADS, tm, HEAD_DIM), head4),
        pl.BlockSpec((1, ATTN_KV_HEADS, VT_ROWS, tm), lambda b, i: (b, 0, 0, i)),
        pl.BlockSpec((1, tm, 512), tok3),
        pl.BlockSpec((1, RET_HEADS, tm, HEAD_DIM), head4),
        pl.BlockSpec((1, RET_HEADS, tm, HEAD_DIM), head4),
        pl.BlockSpec((1, tm, 512), tok3),
        pl.BlockSpec((1, tm, 512), tok3),
        pl.BlockSpec((1, tm, 2048), tok3),
    ]
    return pl.pallas_call(
        functools.partial(_inproj_kernel, tm=tm),
        grid=(bsz, nt),
        in_specs=[pl.BlockSpec((1, tm, d), tok3),
                  pl.BlockSpec((1, 1, d), lambda b, i: (b, 0, 0)),
                  pl.BlockSpec((1, 1, d), lambda b, i: (b, 0, 0)),
                  pl.BlockSpec((d, n), const, pipeline_mode=pl.Buffered(1)),
                  pl.BlockSpec((tm, LANES), lambda b, i: (i, 0)),
                  pl.BlockSpec((tm, LANES), lambda b, i: (i, 0)),
                  pl.BlockSpec((512, 512), const),
                  pl.BlockSpec((1, 512), const),
                  pl.BlockSpec((1, LANES), const)],
        out_specs=out_specs,
        out_shape=out_shape,
        compiler_params=_params("arbitrary", "arbitrary"),
        name="in_proj",
    )(x, mul, shift, w_bf16, cos, sin, ones_blk, qg, kg)


def _attn_kernel(q_ref, k_ref, v_ref, z_ref, o_ref, *, tq, tk, cw, seq, lookahead):
    group = ATTN_HEADS // ATTN_KV_HEADS
    cols = group * tq

    units = [(c, j) for j in range(seq // tk) for c in range(cols // cw)]

    def scores(c, j):
        head, start = divmod(c * cw, tq)
        return _mm(k_ref[0, 0, j * tk:(j + 1) * tk, :], q_ref[0, head, :, start:start + cw])

    m = [jnp.full((1, cw), -jnp.inf, F32)] * (cols // cw)
    acc = [jnp.zeros((VT_ROWS, cw), F32)] * (cols // cw)
    pending = [scores(*u) for u in units[:lookahead]]
    for i, (c, j) in enumerate(units):
        if i + lookahead < len(units):
            pending.append(scores(*units[i + lookahead]))
        st = pending.pop(0)
        m_new = jnp.maximum(m[c], jnp.max(st, axis=0, keepdims=True))
        pt = jnp.exp2(st - m_new).astype(BF16)
        acc[c] = jnp.exp2(m[c] - m_new) * acc[c] + _mm(v_ref[0, 0, :, j * tk:(j + 1) * tk], pt)
        m[c] = m_new
    acc = jnp.concatenate(acc, axis=1)
    out = acc[:HEAD_DIM] / acc[HEAD_DIM:HEAD_DIM + 1]
    out = jnp.concatenate([out[:, h * tq:(h + 1) * tq].T for h in range(group)], axis=1)
    o_ref[0] = (out * z_ref[0].astype(F32)).astype(BF16)


def _attention(qa, ka, va, za, *, tq, tk, cw, lookahead):
    bsz, _, _, s = qa.shape
    group = ATTN_HEADS // ATTN_KV_HEADS
    width = group * HEAD_DIM
    return pl.pallas_call(
        functools.partial(_attn_kernel, tq=tq, tk=tk, cw=cw, seq=s, lookahead=lookahead),
        grid=(bsz, ATTN_KV_HEADS, s // tq),
        in_specs=[pl.BlockSpec((1, group, HEAD_DIM, tq), lambda b, g, i: (b, g, 0, i)),
                  pl.BlockSpec((1, 1, s, HEAD_DIM), lambda b, g, i: (b, g, 0, 0)),
                  pl.BlockSpec((1, 1, VT_ROWS, s), lambda b, g, i: (b, g, 0, 0)),
                  pl.BlockSpec((1, tq, width), lambda b, g, i: (b, i, g))],
        out_specs=pl.BlockSpec((1, tq, width), lambda b, g, i: (b, i, g)),
        out_shape=jax.ShapeDtypeStruct((bsz, s, ATTN_HEADS * HEAD_DIM), BF16),
        compiler_params=_params("arbitrary", "arbitrary", "arbitrary"),
        name="gqa_attention",
    )(qa, ka, va, za)


def _log_sigmoid(w):
    return -(jnp.maximum(-w, 0.0) + jnp.log1p(jnp.exp(-jnp.abs(w))))


def _ret_kernel(q_ref, k_ref, v_ref, z_ref, wf_ref, wb_ref, gn_ref, o_ref, *, chunk, seq, lookahead):
    nchunks = seq // chunk
    lg_f = _log_sigmoid(wf_ref[0])[0:1, 0:1]
    lg_b = _log_sigmoid(wb_ref[0])[0:1, 0:1]

    row = lax.broadcasted_iota(jnp.int32, (chunk, chunk), 0).astype(F32)
    colm = lax.broadcasted_iota(jnp.int32, (chunk, chunk), 1).astype(F32)
    diff = row - colm
    decay = jnp.where(diff >= 0, jnp.exp(lg_f * jnp.maximum(diff, 0.0)), jnp.exp(lg_b * jnp.maximum(-diff, 0.0)))
    idx = lax.broadcasted_iota(jnp.int32, (chunk, HEAD_DIM), 0).astype(F32)
    q_dec_f = jnp.exp(lg_f * (idx + 1.0))
    q_dec_b = jnp.exp(lg_b * (chunk - idx))
    k_dec_f = jnp.exp(lg_f * (chunk - 1.0 - idx))
    k_dec_b = jnp.exp(lg_b * idx)
    chunk_dec_f = jnp.exp(lg_f * chunk)
    chunk_dec_b = jnp.exp(lg_b * chunk)

    def rows(n):
        return slice(n * chunk, (n + 1) * chunk)

    def kv_update(n, k_dec):
        kd = (k_ref[0, 0, rows(n), :].astype(F32) * k_dec).astype(BF16)
        return _mm_tn(kd, v_ref[0, rows(n), :])

    def scan(order, k_dec, chunk_dec):
        updates = {n: kv_update(n, k_dec) for n in order}
        states, state = {}, jnp.zeros((HEAD_DIM, RET_VALUE_DIM), F32)
        for n in order:
            states[n] = state.astype(BF16)
            state = chunk_dec * state + updates[n]
        return states

    state_b = scan(range(nchunks - 1, -1, -1), k_dec_b, chunk_dec_b)
    state_f = scan(range(nchunks), k_dec_f, chunk_dec_f)

    def scores(n):
        return _mm_nt(q_ref[0, 0, rows(n), :], k_ref[0, 0, rows(n), :])

    gn = gn_ref[...]
    pending = [scores(n) for n in range(min(lookahead, nchunks))]
    for n in range(nchunks):
        if n + lookahead < nchunks:
            pending.append(scores(n + lookahead))
        qf = q_ref[0, 0, rows(n), :].astype(F32)
        o = _mm((pending.pop(0) * decay).astype(BF16), v_ref[0, rows(n), :])
        o += _mm((qf * q_dec_f).astype(BF16), state_f[n])
        o += _mm((qf * q_dec_b).astype(BF16), state_b[n])
        mu = jnp.mean(o, axis=-1, keepdims=True)
        cen = o - mu
        var = jnp.mean(cen * cen, axis=-1, keepdims=True)
        y = cen * lax.rsqrt(var + EPS) * gn * z_ref[0, rows(n), :].astype(F32)
        o_ref[0, rows(n), :] = y.astype(BF16)


def _retention(qr, kr, vr, zr, wf, wb, gn, *, chunk, lookahead):
    bsz, heads, s, _ = qr.shape
    head_blk = lambda b, h: (b, h, 0, 0)
    col_blk = lambda b, h: (b, 0, h)
    return pl.pallas_call(
        functools.partial(_ret_kernel, chunk=chunk, seq=s, lookahead=lookahead),
        grid=(bsz, heads),
        in_specs=[pl.BlockSpec((1, 1, s, HEAD_DIM), head_blk),
                  pl.BlockSpec((1, 1, s, HEAD_DIM), head_blk),
                  pl.BlockSpec((1, s, RET_VALUE_DIM), col_blk),
                  pl.BlockSpec((1, s, RET_VALUE_DIM), col_blk),
                  pl.BlockSpec((1, 8, LANES), lambda b, h: (h, 0, 0)),
                  pl.BlockSpec((1, 8, LANES), lambda b, h: (h, 0, 0)),
                  pl.BlockSpec((1, RET_VALUE_DIM), lambda b, h: (0, h))],
        out_specs=pl.BlockSpec((1, s, RET_VALUE_DIM), col_blk),
        out_shape=jax.ShapeDtypeStruct((bsz, s, heads * RET_VALUE_DIM), BF16),
        compiler_params=_params("arbitrary", "arbitrary"),
        name="retention",
    )(qr, kr, vr, zr, wf, wb, gn)


def _outproj_kernel(ya_ref, yr_ref, gl_ref, x_ref, gate_ref, wpa_ref, wpr_ref, wout_ref, gpost_ref, o_ref, *, sub):
    d = x_ref.shape[-1]
    tiles = [pl.ds(r, sub) for r in range(0, x_ref.shape[1], sub)]
    branch = [(_mm(ya_ref[0, rows, :], wpa_ref[...]), _mm(yr_ref[0, rows, :], wpr_ref[...])) for rows in tiles]
    for rows, (pa, pr) in zip(tiles, branch):
        merged = gl_ref[0, rows, :d].astype(F32) * pa + gl_ref[0, rows, d:].astype(F32) * pr
        z = _mm(merged.astype(BF16), wout_ref[...])
        ms = jnp.mean(z * z, axis=-1, keepdims=True)
        y = z * lax.rsqrt(ms + EPS) * gpost_ref[...]
        o_ref[0, rows, :] = x_ref[0, rows, :] + gate_ref[0] * y


def _outproj(ya, yr, gl, x, gate, wpa, wpr, wout, gpost, *, tm, sub):
    bsz, s, d = x.shape
    tok3 = lambda b, i: (b, i, 0)
    const = lambda b, i: (0, 0)
    return pl.pallas_call(
        functools.partial(_outproj_kernel, sub=sub),
        grid=(bsz, s // tm),
        in_specs=[pl.BlockSpec((1, tm, ya.shape[-1]), tok3),
                  pl.BlockSpec((1, tm, yr.shape[-1]), tok3),
                  pl.BlockSpec((1, tm, 2 * d), tok3),
                  pl.BlockSpec((1, tm, d), tok3),
                  pl.BlockSpec((1, 1, d), lambda b, i: (b, 0, 0)),
                  pl.BlockSpec(wpa.shape, const),
                  pl.BlockSpec(wpr.shape, const),
                  pl.BlockSpec(wout.shape, const),
                  pl.BlockSpec((1, d), const)],
        out_specs=pl.BlockSpec((1, tm, d), tok3),
        out_shape=jax.ShapeDtypeStruct((bsz, s, d), F32),
        compiler_params=_params("arbitrary", "arbitrary"),
        name="out_proj",
    )(ya, yr, gl, x, gate, wpa, wpr, wout, gpost)


def _rope_tables(seq):
    t = jnp.arange(seq)
    row = (t // GRID_W).astype(F32)
    colp = (t % GRID_W).astype(F32)
    half = HEAD_DIM // 2
    inv_freq = ROPE_THETA ** (-jnp.arange(0, half, 2, dtype=F32) / half)
    ang_r = row[:, None] * inv_freq[None, :]
    ang_c = colp[:, None] * inv_freq[None, :]
    cos = jnp.concatenate([jnp.cos(ang_r)] * 2 + [jnp.cos(ang_c)] * 2, axis=-1)
    sin = jnp.concatenate([-jnp.sin(ang_r), jnp.sin(ang_r), -jnp.sin(ang_c), jnp.sin(ang_c)], axis=-1)
    return jnp.tile(cos, (1, 2)), jnp.tile(sin, (1, 2))


def _layer(x, mod, g_pre, w_in, qn_g, kn_g, w_dec_f, w_dec_b, gn_g, w_pa, w_pr, w_out, g_post, tables):
    bsz, s, d = x.shape
    shift, scale, gate = jnp.split(mod, 3, axis=-1)
    mul = (g_pre[None, :] * (1.0 + scale)).reshape(bsz, 1, d)
    shift = shift.reshape(bsz, 1, d)
    gate = gate.reshape(bsz, 1, d)

    blk = jnp.arange(512) // HEAD_DIM
    ones_blk = (blk[:, None] == blk[None, :]).astype(BF16)
    qg = jnp.tile(qn_g * (HEAD_DIM ** -0.5 * LOG2E), ATTN_HEADS).reshape(1, 512)
    kg = jnp.tile(kn_g, ATTN_KV_HEADS).reshape(1, LANES)
    cos, sin = tables

    qa, ka, va, za, qr, kr, vr, zr, gl = _inproj(x, mul, shift, w_in.astype(BF16), cos, sin, ones_blk, qg, kg, tm=1024)
    ya = _attention(qa, ka, va, za, tq=512, tk=256, cw=256, lookahead=12)
    wf = jnp.broadcast_to(w_dec_f.astype(F32)[:, None, None], (RET_HEADS, 8, LANES))
    wb = jnp.broadcast_to(w_dec_b.astype(F32)[:, None, None], (RET_HEADS, 8, LANES))
    yr = _retention(qr, kr, vr, zr, wf, wb, gn_g.reshape(1, -1), chunk=256, lookahead=3)
    return _outproj(ya, yr, gl, x, gate, w_pa.astype(BF16), w_pr.astype(BF16), w_out.astype(BF16),
                    g_post.reshape(1, d), tm=1024, sub=512)


def kernel(x, c, w_ada, b_ada, g_pre, w_in, qn_g, kn_g, w_dec_f, w_dec_b, gn_g, w_pa, w_pr, w_out, g_post):
    tables = _rope_tables(x.shape[1])
    for l in range(w_ada.shape[0]):
        mod = _ada(c, w_ada[l], b_ada[l])
        x = _layer(x, mod, g_pre[l], w_in[l], qn_g[l], kn_g[l], w_dec_f[l], w_dec_b[l], gn_g[l],
                   w_pa[l], w_pr[l], w_out[l], g_post[l], tables)
    return x
```

```python
import functools
import math

import jax
import jax.numpy as jnp
from jax import lax
from jax.experimental import pallas as pl
from jax.experimental.pallas import tpu as pltpu

GRID_W = 64
ATTN_HEADS = 8
ATTN_KV_HEADS = 2
HEAD_DIM = 64
RET_HEADS = 4
RET_VALUE_DIM = 128
VT_ROWS = 80
ROPE_THETA = 10000.0
EPS = 1e-6

IN_SIZES = {"qa": 512, "ka": 128, "va": 128, "za": 512, "qr": 256, "kr": 256, "vr": 512, "zr": 512, "gl": 2048}
IN_OFFSETS = dict(zip(IN_SIZES, [sum(list(IN_SIZES.values())[:i]) for i in range(len(IN_SIZES))]))

LANES = 128
VMEM_LIMIT_BYTES = 56 * 1024 * 1024

F32 = jnp.float32
BF16 = jnp.bfloat16
LOG2E = math.log2(math.e)


def _mm(a, b):
    return jnp.dot(a, b, preferred_element_type=F32)


def _sigmoid(x):
    return 1.0 / (1.0 + jnp.exp(-x))


def _params(*semantics):
    return pltpu.CompilerParams(dimension_semantics=semantics, vmem_limit_bytes=VMEM_LIMIT_BYTES)


def _ada_kernel(c_ref, w_ref, b_ref, o_ref):
    c = c_ref[...]
    ca = (c * _sigmoid(c)).astype(BF16)
    o_ref[...] = _mm(ca, w_ref[...].astype(BF16)) + b_ref[...]


def _ada(c, w, b):
    bsz, d = c.shape
    n = w.shape[1]
    return pl.pallas_call(
        _ada_kernel,
        grid=(n // d,),
        in_specs=[pl.BlockSpec((bsz, d), lambda j: (0, 0)),
                  pl.BlockSpec((d, d), lambda j: (0, j)),
                  pl.BlockSpec((1, d), lambda j: (0, j))],
        out_specs=pl.BlockSpec((bsz, d), lambda j: (0, j)),
        out_shape=jax.ShapeDtypeStruct((bsz, n), F32),
        compiler_params=_params("arbitrary"),
        name="ada_mod",
    )(c, w, b.reshape(1, n))


def _rope(x, cos, sin, lo):
    partner = jnp.where(lo, pltpu.roll(x, LANES - 16, 1), pltpu.roll(x, 16, 1))
    return x * cos + partner * sin


def _head_rms(x, ones_blk):
    ms = _mm((x * x).astype(BF16), ones_blk) * (1.0 / HEAD_DIM)
    return x * lax.rsqrt(ms + EPS)


def _inproj_kernel(x_ref, mul_ref, shift_ref, w_ref, cos_ref, sin_ref, ones_ref, qg_ref, kg_ref,
                   qa_ref, ka_ref, va_ref, za_ref, qr_ref, kr_ref, vr_ref, zr_ref, gl_ref, *, tm):
    x = x_ref[0]
    ms = jnp.mean(x * x, axis=-1, keepdims=True)
    h = (x * lax.rsqrt(ms + EPS) * mul_ref[0] + shift_ref[0]).astype(BF16)

    cos = cos_ref[...]
    sin = sin_ref[...]
    lane = lax.broadcasted_iota(jnp.int32, (tm, LANES), 1)
    lo = (lane & 31) < 16
    ones_blk = ones_ref[...]

    def rope_cols(y, scale_row):
        outs = []
        for j in range(y.shape[1] // LANES):
            blk = y[:, j * LANES:(j + 1) * LANES]
            if scale_row is not None:
                blk = blk * scale_row[:, j * LANES:(j + 1) * LANES]
            outs.append(_rope(blk, cos, sin, lo))
        return outs

    def seg(name, part=0, width=None):
        start = IN_OFFSETS[name] + part
        width = IN_SIZES[name] if width is None else width
        return _mm(h, w_ref[:, start:start + width])

    def store_heads(ref, blks):
        for j, blk in enumerate(blks):
            blk = blk.astype(BF16)
            ref[0, 2 * j] = blk[:, :HEAD_DIM]
            ref[0, 2 * j + 1] = blk[:, HEAD_DIM:]

    def store_heads_transposed(ref, blks):
        for j, blk in enumerate(blks):
            blk_t = blk.T.astype(BF16)
            ref[0, 2 * j] = blk_t[:HEAD_DIM]
            ref[0, 2 * j + 1] = blk_t[HEAD_DIM:]

    for j in range(IN_SIZES["gl"] // 512):
        gl_ref[0, :, j * 512:(j + 1) * 512] = _sigmoid(seg("gl", j * 512, 512)).astype(BF16)
    z = seg("za")
    za_ref[0] = (z * _sigmoid(z)).astype(BF16)
    z = seg("zr")
    zr_ref[0] = (z * _sigmoid(z)).astype(BF16)
    q = _head_rms(seg("qa"), ones_blk)
    store_heads_transposed(qa_ref, rope_cols(q, qg_ref[...]))
    kv = seg("ka", 0, IN_SIZES["ka"] + IN_SIZES["va"])
    k = _head_rms(kv[:, :LANES], ones_blk[:LANES, :LANES])
    store_heads(ka_ref, rope_cols(k, kg_ref[...]))
    vt = kv[:, LANES:].T
    sub = lax.broadcasted_iota(jnp.int32, (VT_ROWS - HEAD_DIM, tm), 0)
    ones_rows = jnp.where(sub == 0, 1.0, 0.0)
    for g in range(ATTN_KV_HEADS):
        ext = jnp.concatenate([vt[g * HEAD_DIM:(g + 1) * HEAD_DIM], ones_rows], axis=0)
        va_ref[0, g] = ext.astype(BF16)
    store_heads(qr_ref, rope_cols(seg("qr"), None))
    store_heads_transposed(kr_ref, [blk * HEAD_DIM ** -0.5 for blk in rope_cols(seg("kr"), None)])
    vr_ref[0] = seg("vr").astype(BF16)


def _inproj(x, mul, shift, w_bf16, cos, sin, ones_blk, qg, kg, *, tm):
    bsz, s, d = x.shape
    n = w_bf16.shape[1]
    nt = s // tm
    const = lambda b, i: (0, 0)
    tok3 = lambda b, i: (b, i, 0)
    head4 = lambda b, i: (b, 0, i, 0)
    out_shape = [
        jax.ShapeDtypeStruct((bsz, ATTN_HEADS, HEAD_DIM, s), BF16),
        jax.ShapeDtypeStruct((bsz, ATTN_KV_HEADS, s, HEAD_DIM), BF16),
        jax.ShapeDtypeStruct((bsz, ATTN_KV_HEADS, VT_ROWS, s), BF16),
        jax.ShapeDtypeStruct((bsz, s, 512), BF16),
        jax.ShapeDtypeStruct((bsz, RET_HEADS, s, HEAD_DIM), BF16),
        jax.ShapeDtypeStruct((bsz, RET_HEADS, HEAD_DIM, s), BF16),
        jax.ShapeDtypeStruct((bsz, s, 512), BF16),
        jax.ShapeDtypeStruct((bsz, s, 512), BF16),
        jax.ShapeDtypeStruct((bsz, s, 2048), BF16),
    ]
    out_specs = [
        pl.BlockSpec((1, ATTN_HEADS, HEAD_DIM, tm), lambda b, i: (b, 0, 0, i)),
        pl.BlockSpec((1, ATTN_KV_HEADS, tm, HEAD_DIM), head4),
        pl.BlockSpec((1, ATTN_KV_HEADS, VT_ROWS, tm), lambda b, i: (b, 0, 0, i)),
        pl.BlockSpec((1, tm, 512), tok3),
        pl.BlockSpec((1, RET_HEADS, tm, HEAD_DIM), head4),
        pl.BlockSpec((1, RET_HEADS, HEAD_DIM, tm), lambda b, i: (b, 0, 0, i)),
        pl.BlockSpec((1, tm, 512), tok3),
        pl.BlockSpec((1, tm, 512), tok3),
        pl.BlockSpec((1, tm, 2048), tok3),
    ]
    return pl.pallas_call(
        functools.partial(_inproj_kernel, tm=tm),
        grid=(bsz, nt),
        in_specs=[pl.BlockSpec((1, tm, d), tok3),
                  pl.BlockSpec((1, 1, d), lambda b, i: (b, 0, 0)),
                  pl.BlockSpec((1, 1, d), lambda b, i: (b, 0, 0)),
                  pl.BlockSpec((d, n), const, pipeline_mode=pl.Buffered(1)),
                  pl.BlockSpec((tm, LANES), lambda b, i: (i, 0)),
                  pl.BlockSpec((tm, LANES), lambda b, i: (i, 0)),
                  pl.BlockSpec((512, 512), const),
                  pl.BlockSpec((1, 512), const),
                  pl.BlockSpec((1, LANES), const)],
        out_specs=out_specs,
        out_shape=out_shape,
        compiler_params=_params("arbitrary", "arbitrary"),
        name="in_proj",
    )(x, mul, shift, w_bf16, cos, sin, ones_blk, qg, kg)


def _attn_kernel(q_ref, k_ref, v_ref, z_ref, o_ref, *, tq, tk, cw, blocks_per_group, seq, lookahead):
    group = ATTN_HEADS // ATTN_KV_HEADS
    cols = group * tq

    nblk = cols // cw
    units = [(c, j) for g0 in range(0, nblk, blocks_per_group) for j in range(seq // tk)
             for c in range(g0, g0 + blocks_per_group)]

    def scores(c, j):
        head, start = divmod(c * cw, tq)
        return _mm(k_ref[0, 0, j * tk:(j + 1) * tk, :], q_ref[0, head, :, start:start + cw])

    m = [jnp.full((1, cw), -jnp.inf, F32)] * (cols // cw)
    acc = [jnp.zeros((VT_ROWS, cw), F32)] * (cols // cw)
    pending = [scores(*u) for u in units[:lookahead]]
    for i, (c, j) in enumerate(units):
        if i + lookahead < len(units):
            pending.append(scores(*units[i + lookahead]))
        st = pending.pop(0)
        m_new = jnp.maximum(m[c], jnp.max(st, axis=0, keepdims=True))
        pt = jnp.exp2(st - m_new).astype(BF16)
        acc[c] = jnp.exp2(m[c] - m_new) * acc[c] + _mm(v_ref[0, 0, :, j * tk:(j + 1) * tk], pt)
        m[c] = m_new
    acc = jnp.concatenate(acc, axis=1)
    out = acc[:HEAD_DIM] / acc[HEAD_DIM:HEAD_DIM + 1]
    out = jnp.concatenate([out[:, h * tq:(h + 1) * tq].T for h in range(group)], axis=1)
    o_ref[0] = (out * z_ref[0].astype(F32)).astype(BF16)


def _attention(qa, ka, va, za, *, tq, tk, cw, blocks_per_group, lookahead):
    bsz, _, _, s = qa.shape
    group = ATTN_HEADS // ATTN_KV_HEADS
    width = group * HEAD_DIM
    return pl.pallas_call(
        functools.partial(_attn_kernel, tq=tq, tk=tk, cw=cw, blocks_per_group=blocks_per_group, seq=s,
                          lookahead=lookahead),
        grid=(bsz, ATTN_KV_HEADS, s // tq),
        in_specs=[pl.BlockSpec((1, group, HEAD_DIM, tq), lambda b, g, i: (b, g, 0, i)),
                  pl.BlockSpec((1, 1, s, HEAD_DIM), lambda b, g, i: (b, g, 0, 0)),
                  pl.BlockSpec((1, 1, VT_ROWS, s), lambda b, g, i: (b, g, 0, 0)),
                  pl.BlockSpec((1, tq, width), lambda b, g, i: (b, i, g))],
        out_specs=pl.BlockSpec((1, tq, width), lambda b, g, i: (b, i, g)),
        out_shape=jax.ShapeDtypeStruct((bsz, s, ATTN_HEADS * HEAD_DIM), BF16),
        compiler_params=_params("arbitrary", "arbitrary", "arbitrary"),
        name="gqa_attention",
    )(qa, ka, va, za)


def _log_sigmoid(w):
    return -(jnp.maximum(-w, 0.0) + jnp.log1p(jnp.exp(-jnp.abs(w))))


def _ret_kernel(q_ref, k_ref, v_ref, z_ref, wf_ref, wb_ref, gn_ref, o_ref, *, chunk, seq, lookahead):
    nchunks = seq // chunk
    lg_f = _log_sigmoid(wf_ref[0])[0:1, 0:1]
    lg_b = _log_sigmoid(wb_ref[0])[0:1, 0:1]

    row = lax.broadcasted_iota(jnp.int32, (chunk, chunk), 0).astype(F32)
    colm = lax.broadcasted_iota(jnp.int32, (chunk, chunk), 1).astype(F32)
    diff = row - colm
    decay = jnp.where(diff >= 0, jnp.exp(lg_f * jnp.maximum(diff, 0.0)), jnp.exp(lg_b * jnp.maximum(-diff, 0.0)))
    idx = lax.broadcasted_iota(jnp.int32, (chunk, HEAD_DIM), 0).astype(F32)
    q_dec_f = jnp.exp(lg_f * (idx + 1.0))
    q_dec_b = jnp.exp(lg_b * (chunk - idx))
    idx_t = lax.broadcasted_iota(jnp.int32, (HEAD_DIM, chunk), 1).astype(F32)
    k_dec_f = jnp.exp(lg_f * (chunk - 1.0 - idx_t))
    k_dec_b = jnp.exp(lg_b * idx_t)
    chunk_dec_f = jnp.exp(lg_f * chunk)
    chunk_dec_b = jnp.exp(lg_b * chunk)

    def rows(n):
        return slice(n * chunk, (n + 1) * chunk)

    def kv_update(n, k_dec):
        kd = (k_ref[0, 0, :, rows(n)].astype(F32) * k_dec).astype(BF16)
        return _mm(kd, v_ref[0, rows(n), :])

    def scan(order, k_dec, chunk_dec):
        updates = {n: kv_update(n, k_dec) for n in order}
        states, state = {}, jnp.zeros((HEAD_DIM, RET_VALUE_DIM), F32)
        for n in order:
            states[n] = state.astype(BF16)
            state = chunk_dec * state + updates[n]
        return states

    state_b = scan(range(nchunks - 1, -1, -1), k_dec_b, chunk_dec_b)
    state_f = scan(range(nchunks), k_dec_f, chunk_dec_f)

    def scores(n):
        return _mm(q_ref[0, 0, rows(n), :], k_ref[0, 0, :, rows(n)])

    gn = gn_ref[...]
    pending = [scores(n) for n in range(min(lookahead, nchunks))]
    for n in range(nchunks):
        if n + lookahead < nchunks:
            pending.append(scores(n + lookahead))
        qf = q_ref[0, 0, rows(n), :].astype(F32)
        o = _mm((pending.pop(0) * decay).astype(BF16), v_ref[0, rows(n), :])
        o += _mm((qf * q_dec_f).astype(BF16), state_f[n])
        o += _mm((qf * q_dec_b).astype(BF16), state_b[n])
        mu = jnp.mean(o, axis=-1, keepdims=True)
        cen = o - mu
        var = jnp.mean(cen * cen, axis=-1, keepdims=True)
        y = cen * lax.rsqrt(var + EPS) * gn * z_ref[0, rows(n), :].astype(F32)
        o_ref[0, rows(n), :] = y.astype(BF16)


def _retention(qr, kr, vr, zr, wf, wb, gn, *, chunk, lookahead):
    bsz, heads, s, _ = qr.shape
    head_blk = lambda b, h: (b, h, 0, 0)
    col_blk = lambda b, h: (b, 0, h)
    return pl.pallas_call(
        functools.partial(_ret_kernel, chunk=chunk, seq=s, lookahead=lookahead),
        grid=(bsz, heads),
        in_specs=[pl.BlockSpec((1, 1, s, HEAD_DIM), head_blk),
                  pl.BlockSpec((1, 1, HEAD_DIM, s), head_blk),
                  pl.BlockSpec((1, s, RET_VALUE_DIM), col_blk),
                  pl.BlockSpec((1, s, RET_VALUE_DIM), col_blk),
                  pl.BlockSpec((1, 8, LANES), lambda b, h: (h, 0, 0)),
                  pl.BlockSpec((1, 8, LANES), lambda b, h: (h, 0, 0)),
                  pl.BlockSpec((1, RET_VALUE_DIM), lambda b, h: (0, h))],
        out_specs=pl.BlockSpec((1, s, RET_VALUE_DIM), col_blk),
        out_shape=jax.ShapeDtypeStruct((bsz, s, heads * RET_VALUE_DIM), BF16),
        compiler_params=_params("arbitrary", "arbitrary"),
        name="retention",
    )(qr, kr, vr, zr, wf, wb, gn)


def _outproj_kernel(ya_ref, yr_ref, gl_ref, x_ref, gate_ref, wpa_ref, wpr_ref, wout_ref, gpost_ref, o_ref, *, sub):
    d = x_ref.shape[-1]
    tiles = [pl.ds(r, sub) for r in range(0, x_ref.shape[1], sub)]
    branch = [(_mm(ya_ref[0, rows, :], wpa_ref[...]), _mm(yr_ref[0, rows, :], wpr_ref[...])) for rows in tiles]
    for rows, (pa, pr) in zip(tiles, branch):
        merged = gl_ref[0, rows, :d].astype(F32) * pa + gl_ref[0, rows, d:].astype(F32) * pr
        z = _mm(merged.astype(BF16), wout_ref[...])
        ms = jnp.mean(z * z, axis=-1, keepdims=True)
        y = z * lax.rsqrt(ms + EPS) * gpost_ref[...]
        o_ref[0, rows, :] = x_ref[0, rows, :] + gate_ref[0] * y


def _outproj(ya, yr, gl, x, gate, wpa, wpr, wout, gpost, *, tm, sub):
    bsz, s, d = x.shape
    tok3 = lambda b, i: (b, i, 0)
    const = lambda b, i: (0, 0)
    return pl.pallas_call(
        functools.partial(_outproj_kernel, sub=sub),
        grid=(bsz, s // tm),
        in_specs=[pl.BlockSpec((1, tm, ya.shape[-1]), tok3),
                  pl.BlockSpec((1, tm, yr.shape[-1]), tok3),
                  pl.BlockSpec((1, tm, 2 * d), tok3),
                  pl.BlockSpec((1, tm, d), tok3),
                  pl.BlockSpec((1, 1, d), lambda b, i: (b, 0, 0)),
                  pl.BlockSpec(wpa.shape, const),
                  pl.BlockSpec(wpr.shape, const),
                  pl.BlockSpec(wout.shape, const),
                  pl.BlockSpec((1, d), const)],
        out_specs=pl.BlockSpec((1, tm, d), tok3),
        out_shape=jax.ShapeDtypeStruct((bsz, s, d), F32),
        compiler_params=_params("arbitrary", "arbitrary"),
        name="out_proj",
    )(ya, yr, gl, x, gate, wpa, wpr, wout, gpost)


def _rope_tables(seq):
    t = jnp.arange(seq)
    row = (t // GRID_W).astype(F32)
    colp = (t % GRID_W).astype(F32)
    half = HEAD_DIM // 2
    inv_freq = ROPE_THETA ** (-jnp.arange(0, half, 2, dtype=F32) / half)
    ang_r = row[:, None] * inv_freq[None, :]
    ang_c = colp[:, None] * inv_freq[None, :]
    cos = jnp.concatenate([jnp.cos(ang_r)] * 2 + [jnp.cos(ang_c)] * 2, axis=-1)
    sin = jnp.concatenate([-jnp.sin(ang_r), jnp.sin(ang_r), -jnp.sin(ang_c), jnp.sin(ang_c)], axis=-1)
    return jnp.tile(cos, (1, 2)), jnp.tile(sin, (1, 2))


def _layer(x, mod, g_pre, w_in, qn_g, kn_g, w_dec_f, w_dec_b, gn_g, w_pa, w_pr, w_out, g_post, tables):
    bsz, s, d = x.shape
    shift, scale, gate = jnp.split(mod, 3, axis=-1)
    mul = (g_pre[None, :] * (1.0 + scale)).reshape(bsz, 1, d)
    shift = shift.reshape(bsz, 1, d)
    gate = gate.reshape(bsz, 1, d)

    blk = jnp.arange(512) // HEAD_DIM
    ones_blk = (blk[:, None] == blk[None, :]).astype(BF16)
    qg = jnp.tile(qn_g * (HEAD_DIM ** -0.5 * LOG2E), ATTN_HEADS).reshape(1, 512)
    kg = jnp.tile(kn_g, ATTN_KV_HEADS).reshape(1, LANES)
    cos, sin = tables

    qa, ka, va, za, qr, kr, vr, zr, gl = _inproj(x, mul, shift, w_in.astype(BF16), cos, sin, ones_blk, qg, kg, tm=1024)
    ya = _attention(qa, ka, va, za, tq=512, tk=256, cw=256, blocks_per_group=2, lookahead=8)
    wf = jnp.broadcast_to(w_dec_f.astype(F32)[:, None, None], (RET_HEADS, 8, LANES))
    wb = jnp.broadcast_to(w_dec_b.astype(F32)[:, None, None], (RET_HEADS, 8, LANES))
    yr = _retention(qr, kr, vr, zr, wf, wb, gn_g.reshape(1, -1), chunk=256, lookahead=3)
    return _outproj(ya, yr, gl, x, gate, w_pa.astype(BF16), w_pr.astype(BF16), w_out.astype(BF16),
                    g_post.reshape(1, d), tm=1024, sub=512)


def kernel(x, c, w_ada, b_ada, g_pre, w_in, qn_g, kn_g, w_dec_f, w_dec_b, gn_g, w_pa, w_pr, w_out, g_post):
    tables = _rope_tables(x.shape[1])
    for l in range(w_ada.shape[0]):
        mod = _ada(c, w_ada[l], b_ada[l])
        x = _layer(x, mod, g_pre[l], w_in[l], qn_g[l], kn_g[l], w_dec_f[l], w_dec_b[l], gn_g[l],
                   w_pa[l], w_pr[l], w_out[l], g_post[l], tables)
    return x
```

```python
import functools
import math

import jax
import jax.numpy as jnp
from jax import lax
from jax.experimental import pallas as pl
from jax.experimental.pallas import tpu as pltpu

GRID_W = 64
ATTN_HEADS = 8
ATTN_KV_HEADS = 2
HEAD_DIM = 64
RET_HEADS = 4
RET_VALUE_DIM = 128
VT_ROWS = 80
ROPE_THETA = 10000.0
EPS = 1e-6

IN_SIZES = {"qa": 512, "ka": 128, "va": 128, "za": 512, "qr": 256, "kr": 256, "vr": 512, "zr": 512, "gl": 2048}
IN_OFFSETS = dict(zip(IN_SIZES, [sum(list(IN_SIZES.values())[:i]) for i in range(len(IN_SIZES))]))

LANES = 128
VMEM_LIMIT_BYTES = 56 * 1024 * 1024

F32 = jnp.float32
BF16 = jnp.bfloat16
LOG2E = math.log2(math.e)


def _mm(a, b):
    return jnp.dot(a, b, preferred_element_type=F32)


def _sigmoid(x):
    return 1.0 / (1.0 + jnp.exp(-x))


def _params(*semantics):
    return pltpu.CompilerParams(dimension_semantics=semantics, vmem_limit_bytes=VMEM_LIMIT_BYTES)


def _ada_kernel(c_ref, w_ref, b_ref, o_ref):
    c = c_ref[...]
    ca = (c * _sigmoid(c)).astype(BF16)
    o_ref[...] = _mm(ca, w_ref[...].astype(BF16)) + b_ref[...]


def _ada(c, w, b):
    bsz, d = c.shape
    n = w.shape[1]
    return pl.pallas_call(
        _ada_kernel,
        grid=(n // d,),
        in_specs=[pl.BlockSpec((bsz, d), lambda j: (0, 0)),
                  pl.BlockSpec((d, d), lambda j: (0, j)),
                  pl.BlockSpec((1, d), lambda j: (0, j))],
        out_specs=pl.BlockSpec((bsz, d), lambda j: (0, j)),
        out_shape=jax.ShapeDtypeStruct((bsz, n), F32),
        compiler_params=_params("arbitrary"),
        name="ada_mod",
    )(c, w, b.reshape(1, n))


def _rope(x, cos, sin, lo):
    partner = jnp.where(lo, pltpu.roll(x, LANES - 16, 1), pltpu.roll(x, 16, 1))
    return x * cos + partner * sin


def _head_rms(x, ones_blk):
    ms = _mm((x * x).astype(BF16), ones_blk) * (1.0 / HEAD_DIM)
    return x * lax.rsqrt(ms + EPS)


def _inproj_kernel(x_ref, mul_ref, shift_ref, w_ref, cos_ref, sin_ref, ones_ref, qg_ref, kg_ref,
                   qa_ref, ka_ref, va_ref, za_ref, qr_ref, kr_ref, vr_ref, zr_ref, gl_ref, *, tm, sub):
    for r0 in range(0, tm, sub):
        _inproj_rows(slice(r0, r0 + sub), x_ref, mul_ref, shift_ref, w_ref, cos_ref, sin_ref, ones_ref, qg_ref, kg_ref,
                     qa_ref, ka_ref, va_ref, za_ref, qr_ref, kr_ref, vr_ref, zr_ref, gl_ref)


def _inproj_rows(rows, x_ref, mul_ref, shift_ref, w_ref, cos_ref, sin_ref, ones_ref, qg_ref, kg_ref,
                 qa_ref, ka_ref, va_ref, za_ref, qr_ref, kr_ref, vr_ref, zr_ref, gl_ref):
    nrows = rows.stop - rows.start
    x = x_ref[0, rows, :]
    ms = jnp.mean(x * x, axis=-1, keepdims=True)
    h = (x * lax.rsqrt(ms + EPS) * mul_ref[0] + shift_ref[0]).astype(BF16)

    cos = cos_ref[rows, :]
    sin = sin_ref[rows, :]
    lane = lax.broadcasted_iota(jnp.int32, (nrows, LANES), 1)
    lo = (lane & 31) < 16
    ones_blk = ones_ref[...]

    def rope_cols(y, scale_row):
        outs = []
        for j in range(y.shape[1] // LANES):
            blk = y[:, j * LANES:(j + 1) * LANES]
            if scale_row is not None:
                blk = blk * scale_row[:, j * LANES:(j + 1) * LANES]
            outs.append(_rope(blk, cos, sin, lo))
        return outs

    def seg(name, part=0, width=None):
        start = IN_OFFSETS[name] + part
        width = IN_SIZES[name] if width is None else width
        return _mm(h, w_ref[:, start:start + width])

    def store_heads(ref, blks):
        for j, blk in enumerate(blks):
            blk = blk.astype(BF16)
            ref[0, 2 * j, rows, :] = blk[:, :HEAD_DIM]
            ref[0, 2 * j + 1, rows, :] = blk[:, HEAD_DIM:]

    def store_heads_transposed(ref, blks):
        for j, blk in enumerate(blks):
            blk_t = blk.T.astype(BF16)
            ref[0, 2 * j, :, rows] = blk_t[:HEAD_DIM]
            ref[0, 2 * j + 1, :, rows] = blk_t[HEAD_DIM:]

    for j in range(IN_SIZES["gl"] // 512):
        gl_ref[0, rows, j * 512:(j + 1) * 512] = _sigmoid(seg("gl", j * 512, 512)).astype(BF16)
    z = seg("za")
    za_ref[0, rows, :] = (z * _sigmoid(z)).astype(BF16)
    z = seg("zr")
    zr_ref[0, rows, :] = (z * _sigmoid(z)).astype(BF16)
    q = _head_rms(seg("qa"), ones_blk)
    store_heads_transposed(qa_ref, rope_cols(q, qg_ref[...]))
    kv = seg("ka", 0, IN_SIZES["ka"] + IN_SIZES["va"])
    k = _head_rms(kv[:, :LANES], ones_blk[:LANES, :LANES])
    store_heads(ka_ref, rope_cols(k, kg_ref[...]))
    vt = kv[:, LANES:].T
    ones_rows = jnp.where(lax.broadcasted_iota(jnp.int32, (VT_ROWS - HEAD_DIM, nrows), 0) == 0, 1.0, 0.0)
    for g in range(ATTN_KV_HEADS):
        ext = jnp.concatenate([vt[g * HEAD_DIM:(g + 1) * HEAD_DIM], ones_rows], axis=0)
        va_ref[0, g, :, rows] = ext.astype(BF16)
    store_heads(qr_ref, rope_cols(seg("qr"), None))
    store_heads_transposed(kr_ref, [blk * HEAD_DIM ** -0.5 for blk in rope_cols(seg("kr"), None)])
    vr_ref[0, rows, :] = seg("vr").astype(BF16)


def _inproj(x, mul, shift, w_bf16, cos, sin, ones_blk, qg, kg, *, tm, sub):
    bsz, s, d = x.shape
    n = w_bf16.shape[1]
    nt = s // tm
    const = lambda b, i: (0, 0)
    tok3 = lambda b, i: (b, i, 0)
    head4 = lambda b, i: (b, 0, i, 0)
    out_shape = [
        jax.ShapeDtypeStruct((bsz, ATTN_HEADS, HEAD_DIM, s), BF16),
        jax.ShapeDtypeStruct((bsz, ATTN_KV_HEADS, s, HEAD_DIM), BF16),
        jax.ShapeDtypeStruct((bsz, ATTN_KV_HEADS, VT_ROWS, s), BF16),
        jax.ShapeDtypeStruct((bsz, s, 512), BF16),
        jax.ShapeDtypeStruct((bsz, RET_HEADS, s, HEAD_DIM), BF16),
        jax.ShapeDtypeStruct((bsz, RET_HEADS, HEAD_DIM, s), BF16),
        jax.ShapeDtypeStruct((bsz, s, 512), BF16),
        jax.ShapeDtypeStruct((bsz, s, 512), BF16),
        jax.ShapeDtypeStruct((bsz, s, 2048), BF16),
    ]
    out_specs = [
        pl.BlockSpec((1, ATTN_HEADS, HEAD_DIM, tm), lambda b, i: (b, 0, 0, i)),
        pl.BlockSpec((1, ATTN_KV_HEADS, tm, HEAD_DIM), head4),
        pl.BlockSpec((1, ATTN_KV_HEADS, VT_ROWS, tm), lambda b, i: (b, 0, 0, i)),
        pl.BlockSpec((1, tm, 512), tok3),
        pl.BlockSpec((1, RET_HEADS, tm, HEAD_DIM), head4),
        pl.BlockSpec((1, RET_HEADS, HEAD_DIM, tm), lambda b, i: (b, 0, 0, i)),
        pl.BlockSpec((1, tm, 512), tok3),
        pl.BlockSpec((1, tm, 512), tok3),
        pl.BlockSpec((1, tm, 2048), tok3),
    ]
    return pl.pallas_call(
        functools.partial(_inproj_kernel, tm=tm, sub=sub),
        grid=(bsz, nt),
        in_specs=[pl.BlockSpec((1, tm, d), tok3),
                  pl.BlockSpec((1, 1, d), lambda b, i: (b, 0, 0)),
                  pl.BlockSpec((1, 1, d), lambda b, i: (b, 0, 0)),
                  pl.BlockSpec((d, n), const, pipeline_mode=pl.Buffered(1)),
                  pl.BlockSpec((tm, LANES), lambda b, i: (i, 0)),
                  pl.BlockSpec((tm, LANES), lambda b, i: (i, 0)),
                  pl.BlockSpec((512, 512), const),
                  pl.BlockSpec((1, 512), const),
                  pl.BlockSpec((1, LANES), const)],
        out_specs=out_specs,
        out_shape=out_shape,
        compiler_params=_params("arbitrary", "arbitrary"),
        name="in_proj",
    )(x, mul, shift, w_bf16, cos, sin, ones_blk, qg, kg)


def _attn_kernel(q_ref, k_ref, v_ref, z_ref, o_ref, *, tq, tk, cw, blocks_per_group, seq, lookahead):
    group = ATTN_HEADS // ATTN_KV_HEADS
    cols = group * tq

    nblk = cols // cw
    units = [(c, j) for g0 in range(0, nblk, blocks_per_group) for j in range(seq // tk)
             for c in range(g0, g0 + blocks_per_group)]

    def scores(c, j):
        head, start = divmod(c * cw, tq)
        return _mm(k_ref[0, 0, j * tk:(j + 1) * tk, :], q_ref[0, head, :, start:start + cw])

    m = [jnp.full((1, cw), -jnp.inf, F32)] * (cols // cw)
    acc = [jnp.zeros((VT_ROWS, cw), F32)] * (cols // cw)
    pending = [scores(*u) for u in units[:lookahead]]
    for i, (c, j) in enumerate(units):
        if i + lookahead < len(units):
            pending.append(scores(*units[i + lookahead]))
        st = pending.pop(0)
        m_new = jnp.maximum(m[c], jnp.max(st, axis=0, keepdims=True))
        pt = jnp.exp2(st - m_new).astype(BF16)
        acc[c] = jnp.exp2(m[c] - m_new) * acc[c] + _mm(v_ref[0, 0, :, j * tk:(j + 1) * tk], pt)
        m[c] = m_new
    acc = jnp.concatenate(acc, axis=1)
    out = acc[:HEAD_DIM] / acc[HEAD_DIM:HEAD_DIM + 1]
    out = jnp.concatenate([out[:, h * tq:(h + 1) * tq].T for h in range(group)], axis=1)
    o_ref[0] = (out * z_ref[0].astype(F32)).astype(BF16)


def _attention(qa, ka, va, za, *, tq, tk, cw, blocks_per_group, lookahead):
    bsz, _, _, s = qa.shape
    group = ATTN_HEADS // ATTN_KV_HEADS
    width = group * HEAD_DIM
    return pl.pallas_call(
        functools.partial(_attn_kernel, tq=tq, tk=tk, cw=cw, blocks_per_group=blocks_per_group, seq=s,
                          lookahead=lookahead),
        grid=(bsz, ATTN_KV_HEADS, s // tq),
        in_specs=[pl.BlockSpec((1, group, HEAD_DIM, tq), lambda b, g, i: (b, g, 0, i)),
                  pl.BlockSpec((1, 1, s, HEAD_DIM), lambda b, g, i: (b, g, 0, 0)),
                  pl.BlockSpec((1, 1, VT_ROWS, s), lambda b, g, i: (b, g, 0, 0)),
                  pl.BlockSpec((1, tq, width), lambda b, g, i: (b, i, g))],
        out_specs=pl.BlockSpec((1, tq, width), lambda b, g, i: (b, i, g)),
        out_shape=jax.ShapeDtypeStruct((bsz, s, ATTN_HEADS * HEAD_DIM), BF16),
        compiler_params=_params("arbitrary", "arbitrary", "arbitrary"),
        name="gqa_attention",
    )(qa, ka, va, za)


def _log_sigmoid(w):
    return -(jnp.maximum(-w, 0.0) + jnp.log1p(jnp.exp(-jnp.abs(w))))


def _ret_kernel(q_ref, k_ref, v_ref, z_ref, wf_ref, wb_ref, gn_ref, o_ref, *, chunk, seq, lookahead):
    nchunks = seq // chunk
    lg_f = _log_sigmoid(wf_ref[0])[0:1, 0:1]
    lg_b = _log_sigmoid(wb_ref[0])[0:1, 0:1]

    row = lax.broadcasted_iota(jnp.int32, (chunk, chunk), 0).astype(F32)
    colm = lax.broadcasted_iota(jnp.int32, (chunk, chunk), 1).astype(F32)
    diff = row - colm
    decay = jnp.where(diff >= 0, jnp.exp(lg_f * jnp.maximum(diff, 0.0)), jnp.exp(lg_b * jnp.maximum(-diff, 0.0)))
    idx = lax.broadcasted_iota(jnp.int32, (chunk, HEAD_DIM), 0).astype(F32)
    q_dec_f = jnp.exp(lg_f * (idx + 1.0))
    q_dec_b = jnp.exp(lg_b * (chunk - idx))
    idx_t = lax.broadcasted_iota(jnp.int32, (HEAD_DIM, chunk), 1).astype(F32)
    k_dec_f = jnp.exp(lg_f * (chunk - 1.0 - idx_t))
    k_dec_b = jnp.exp(lg_b * idx_t)
    chunk_dec_f = jnp.exp(lg_f * chunk)
    chunk_dec_b = jnp.exp(lg_b * chunk)

    def rows(n):
        return slice(n * chunk, (n + 1) * chunk)

    def kv_update(n, k_dec):
        kd = (k_ref[0, 0, :, rows(n)].astype(F32) * k_dec).astype(BF16)
        return _mm(kd, v_ref[0, rows(n), :])

    def scan(order, k_dec, chunk_dec):
        updates = {n: kv_update(n, k_dec) for n in order}
        states, state = {}, jnp.zeros((HEAD_DIM, RET_VALUE_DIM), F32)
        for n in order:
            states[n] = state.astype(BF16)
            state = chunk_dec * state + updates[n]
        return states

    state_b = scan(range(nchunks - 1, -1, -1), k_dec_b, chunk_dec_b)
    state_f = scan(range(nchunks), k_dec_f, chunk_dec_f)

    def scores(n):
        return _mm(q_ref[0, 0, rows(n), :], k_ref[0, 0, :, rows(n)])

    gn = gn_ref[...]
    pending = [scores(n) for n in range(min(lookahead, nchunks))]
    for n in range(nchunks):
        if n + lookahead < nchunks:
            pending.append(scores(n + lookahead))
        qf = q_ref[0, 0, rows(n), :].astype(F32)
        o = _mm((pending.pop(0) * decay).astype(BF16), v_ref[0, rows(n), :])
        o += _mm((qf * q_dec_f).astype(BF16), state_f[n])
        o += _mm((qf * q_dec_b).astype(BF16), state_b[n])
        mu = jnp.mean(o, axis=-1, keepdims=True)
        cen = o - mu
        var = jnp.mean(cen * cen, axis=-1, keepdims=True)
        y = cen * lax.rsqrt(var + EPS) * gn * z_ref[0, rows(n), :].astype(F32)
        o_ref[0, rows(n), :] = y.astype(BF16)


def _retention(qr, kr, vr, zr, wf, wb, gn, *, chunk, lookahead):
    bsz, heads, s, _ = qr.shape
    head_blk = lambda b, h: (b, h, 0, 0)
    col_blk = lambda b, h: (b, 0, h)
    return pl.pallas_call(
        functools.partial(_ret_kernel, chunk=chunk, seq=s, lookahead=lookahead),
        grid=(bsz, heads),
        in_specs=[pl.BlockSpec((1, 1, s, HEAD_DIM), head_blk),
                  pl.BlockSpec((1, 1, HEAD_DIM, s), head_blk),
                  pl.BlockSpec((1, s, RET_VALUE_DIM), col_blk),
                  pl.BlockSpec((1, s, RET_VALUE_DIM), col_blk),
                  pl.BlockSpec((1, 8, LANES), lambda b, h: (h, 0, 0)),
                  pl.BlockSpec((1, 8, LANES), lambda b, h: (h, 0, 0)),
                  pl.BlockSpec((1, RET_VALUE_DIM), lambda b, h: (0, h))],
        out_specs=pl.BlockSpec((1, s, RET_VALUE_DIM), col_blk),
        out_shape=jax.ShapeDtypeStruct((bsz, s, heads * RET_VALUE_DIM), BF16),
        compiler_params=_params("arbitrary", "arbitrary"),
        name="retention",
    )(qr, kr, vr, zr, wf, wb, gn)


def _outproj_kernel(ya_ref, yr_ref, gl_ref, x_ref, gate_ref, wpa_ref, wpr_ref, wout_ref, gpost_ref, o_ref, *, sub):
    d = x_ref.shape[-1]
    tiles = [pl.ds(r, sub) for r in range(0, x_ref.shape[1], sub)]
    branch = [(_mm(ya_ref[0, rows, :], wpa_ref[...]), _mm(yr_ref[0, rows, :], wpr_ref[...])) for rows in tiles]
    for rows, (pa, pr) in zip(tiles, branch):
        merged = gl_ref[0, rows, :d].astype(F32) * pa + gl_ref[0, rows, d:].astype(F32) * pr
        z = _mm(merged.astype(BF16), wout_ref[...])
        ms = jnp.mean(z * z, axis=-1, keepdims=True)
        y = z * lax.rsqrt(ms + EPS) * gpost_ref[...]
        o_ref[0, rows, :] = x_ref[0, rows, :] + gate_ref[0] * y


def _outproj(ya, yr, gl, x, gate, wpa, wpr, wout, gpost, *, tm, sub):
    bsz, s, d = x.shape
    tok3 = lambda b, i: (b, i, 0)
    const = lambda b, i: (0, 0)
    return pl.pallas_call(
        functools.partial(_outproj_kernel, sub=sub),
        grid=(bsz, s // tm),
        in_specs=[pl.BlockSpec((1, tm, ya.shape[-1]), tok3),
                  pl.BlockSpec((1, tm, yr.shape[-1]), tok3),
                  pl.BlockSpec((1, tm, 2 * d), tok3),
                  pl.BlockSpec((1, tm, d), tok3),
                  pl.BlockSpec((1, 1, d), lambda b, i: (b, 0, 0)),
                  pl.BlockSpec(wpa.shape, const),
                  pl.BlockSpec(wpr.shape, const),
                  pl.BlockSpec(wout.shape, const),
                  pl.BlockSpec((1, d), const)],
        out_specs=pl.BlockSpec((1, tm, d), tok3),
        out_shape=jax.ShapeDtypeStruct((bsz, s, d), F32),
        compiler_params=_params("arbitrary", "arbitrary"),
        name="out_proj",
    )(ya, yr, gl, x, gate, wpa, wpr, wout, gpost)


def _rope_tables(seq):
    t = jnp.arange(seq)
    row = (t // GRID_W).astype(F32)
    colp = (t % GRID_W).astype(F32)
    half = HEAD_DIM // 2
    inv_freq = ROPE_THETA ** (-jnp.arange(0, half, 2, dtype=F32) / half)
    ang_r = row[:, None] * inv_freq[None, :]
    ang_c = colp[:, None] * inv_freq[None, :]
    cos = jnp.concatenate([jnp.cos(ang_r)] * 2 + [jnp.cos(ang_c)] * 2, axis=-1)
    sin = jnp.concatenate([-jnp.sin(ang_r), jnp.sin(ang_r), -jnp.sin(ang_c), jnp.sin(ang_c)], axis=-1)
    return jnp.tile(cos, (1, 2)), jnp.tile(sin, (1, 2))


def _layer(x, mod, g_pre, w_in, qn_g, kn_g, w_dec_f, w_dec_b, gn_g, w_pa, w_pr, w_out, g_post, tables):
    bsz, s, d = x.shape
    shift, scale, gate = jnp.split(mod, 3, axis=-1)
    mul = (g_pre[None, :] * (1.0 + scale)).reshape(bsz, 1, d)
    shift = shift.reshape(bsz, 1, d)
    gate = gate.reshape(bsz, 1, d)

    blk = jnp.arange(512) // HEAD_DIM
    ones_blk = (blk[:, None] == blk[None, :]).astype(BF16)
    qg = jnp.tile(qn_g * (HEAD_DIM ** -0.5 * LOG2E), ATTN_HEADS).reshape(1, 512)
    kg = jnp.tile(kn_g, ATTN_KV_HEADS).reshape(1, LANES)
    cos, sin = tables

    qa, ka, va, za, qr, kr, vr, zr, gl = _inproj(x, mul, shift, w_in.astype(BF16), cos, sin, ones_blk, qg, kg, tm=1024, sub=512)
    ya = _attention(qa, ka, va, za, tq=512, tk=256, cw=256, blocks_per_group=2, lookahead=8)
    wf = jnp.broadcast_to(w_dec_f.astype(F32)[:, None, None], (RET_HEADS, 8, LANES))
    wb = jnp.broadcast_to(w_dec_b.astype(F32)[:, None, None], (RET_HEADS, 8, LANES))
    yr = _retention(qr, kr, vr, zr, wf, wb, gn_g.reshape(1, -1), chunk=256, lookahead=3)
    return _outproj(ya, yr, gl, x, gate, w_pa.astype(BF16), w_pr.astype(BF16), w_out.astype(BF16),
                    g_post.reshape(1, d), tm=1024, sub=512)


def kernel(x, c, w_ada, b_ada, g_pre, w_in, qn_g, kn_g, w_dec_f, w_dec_b, gn_g, w_pa, w_pr, w_out, g_post):
    tables = _rope_tables(x.shape[1])
    for l in range(w_ada.shape[0]):
        mod = _ada(c, w_ada[l], b_ada[l])
        x = _layer(x, mod, g_pre[l], w_in[l], qn_g[l], kn_g[l], w_dec_f[l], w_dec_b[l], gn_g[l],
                   w_pa[l], w_pr[l], w_out[l], g_post[l], tables)
    return x
```

```python
import functools
import math

import jax
import jax.numpy as jnp
from jax import lax
from jax.experimental import pallas as pl
from jax.experimental.pallas import tpu as pltpu

GRID_W = 64
ATTN_HEADS = 8
ATTN_KV_HEADS = 2
HEAD_DIM = 64
RET_HEADS = 4
RET_VALUE_DIM = 128
VT_ROWS = 80
ROPE_THETA = 10000.0
EPS = 1e-6

IN_SIZES = {"qa": 512, "ka": 128, "va": 128, "za": 512, "qr": 256, "kr": 256, "vr": 512, "zr": 512, "gl": 2048}
IN_OFFSETS = dict(zip(IN_SIZES, [sum(list(IN_SIZES.values())[:i]) for i in range(len(IN_SIZES))]))

LANES = 128
VMEM_LIMIT_BYTES = 56 * 1024 * 1024

F32 = jnp.float32
BF16 = jnp.bfloat16
LOG2E = math.log2(math.e)


def _mm(a, b):
    return jnp.dot(a, b, preferred_element_type=F32)


def _sigmoid(x):
    return 1.0 / (1.0 + jnp.exp(-x))


def _params(*semantics):
    return pltpu.CompilerParams(dimension_semantics=semantics, vmem_limit_bytes=VMEM_LIMIT_BYTES)


def _ada_kernel(c_ref, w_ref, b_ref, o_ref):
    c = c_ref[...]
    ca = (c * _sigmoid(c)).astype(BF16)
    o_ref[...] = _mm(ca, w_ref[...].astype(BF16)) + b_ref[...]


def _ada(c, w, b):
    bsz, d = c.shape
    n = w.shape[1]
    return pl.pallas_call(
        _ada_kernel,
        grid=(n // d,),
        in_specs=[pl.BlockSpec((bsz, d), lambda j: (0, 0)),
                  pl.BlockSpec((d, d), lambda j: (0, j)),
                  pl.BlockSpec((1, d), lambda j: (0, j))],
        out_specs=pl.BlockSpec((bsz, d), lambda j: (0, j)),
        out_shape=jax.ShapeDtypeStruct((bsz, n), F32),
        compiler_params=_params("arbitrary"),
        name="ada_mod",
    )(c, w, b.reshape(1, n))


def _rope(x, cos, sin, lo):
    partner = jnp.where(lo, pltpu.roll(x, LANES - 16, 1), pltpu.roll(x, 16, 1))
    return x * cos + partner * sin


def _head_rms(x, ones_blk):
    ms = _mm((x * x).astype(BF16), ones_blk) * (1.0 / HEAD_DIM)
    return x * lax.rsqrt(ms + EPS)


def _inproj_kernel(x_ref, mul_ref, shift_ref, w_ref, cos_ref, sin_ref, ones_ref, qg_ref, kg_ref,
                   qa_ref, ka_ref, va_ref, za_ref, qr_ref, kr_ref, vr_ref, zr_ref, gl_ref, *, tm, sub):
    for r0 in range(0, tm, sub):
        _inproj_rows(slice(r0, r0 + sub), x_ref, mul_ref, shift_ref, w_ref, cos_ref, sin_ref, ones_ref, qg_ref, kg_ref,
                     qa_ref, ka_ref, va_ref, za_ref, qr_ref, kr_ref, vr_ref, zr_ref, gl_ref)


def _inproj_rows(rows, x_ref, mul_ref, shift_ref, w_ref, cos_ref, sin_ref, ones_ref, qg_ref, kg_ref,
                 qa_ref, ka_ref, va_ref, za_ref, qr_ref, kr_ref, vr_ref, zr_ref, gl_ref):
    nrows = rows.stop - rows.start
    x = x_ref[0, rows, :]
    ms = jnp.mean(x * x, axis=-1, keepdims=True)
    h = (x * lax.rsqrt(ms + EPS) * mul_ref[0] + shift_ref[0]).astype(BF16)

    cos = cos_ref[rows, :]
    sin = sin_ref[rows, :]
    lane = lax.broadcasted_iota(jnp.int32, (nrows, LANES), 1)
    lo = (lane & 31) < 16
    ones_blk = ones_ref[...]

    def rope_cols(y, scale_row):
        outs = []
        for j in range(y.shape[1] // LANES):
            blk = y[:, j * LANES:(j + 1) * LANES]
            if scale_row is not None:
                blk = blk * scale_row[:, j * LANES:(j + 1) * LANES]
            outs.append(_rope(blk, cos, sin, lo))
        return outs

    def seg(name, part=0, width=None):
        start = IN_OFFSETS[name] + part
        width = IN_SIZES[name] if width is None else width
        return _mm(h, w_ref[:, start:start + width])

    def store_heads(ref, blks):
        for j, blk in enumerate(blks):
            blk = blk.astype(BF16)
            ref[0, 2 * j, rows, :] = blk[:, :HEAD_DIM]
            ref[0, 2 * j + 1, rows, :] = blk[:, HEAD_DIM:]

    def store_heads_transposed(ref, blks):
        for j, blk in enumerate(blks):
            blk_t = blk.T.astype(BF16)
            ref[0, 2 * j, :, rows] = blk_t[:HEAD_DIM]
            ref[0, 2 * j + 1, :, rows] = blk_t[HEAD_DIM:]

    for j in range(IN_SIZES["gl"] // 512):
        gl_ref[0, rows, j * 512:(j + 1) * 512] = _sigmoid(seg("gl", j * 512, 512)).astype(BF16)
    z = seg("za")
    za_ref[0, rows, :] = (z * _sigmoid(z)).astype(BF16)
    z = seg("zr")
    zr_ref[0, rows, :] = (z * _sigmoid(z)).astype(BF16)
    q = _head_rms(seg("qa"), ones_blk)
    store_heads_transposed(qa_ref, rope_cols(q, qg_ref[...]))
    kv = seg("ka", 0, IN_SIZES["ka"] + IN_SIZES["va"])
    k = _head_rms(kv[:, :LANES], ones_blk[:LANES, :LANES])
    store_heads(ka_ref, rope_cols(k, kg_ref[...]))
    vt = kv[:, LANES:].T
    ones_rows = jnp.where(lax.broadcasted_iota(jnp.int32, (VT_ROWS - HEAD_DIM, nrows), 0) == 0, 1.0, 0.0)
    for g in range(ATTN_KV_HEADS):
        ext = jnp.concatenate([vt[g * HEAD_DIM:(g + 1) * HEAD_DIM], ones_rows], axis=0)
        va_ref[0, g, :, rows] = ext.astype(BF16)
    store_heads(qr_ref, rope_cols(seg("qr"), None))
    store_heads_transposed(kr_ref, [blk * HEAD_DIM ** -0.5 for blk in rope_cols(seg("kr"), None)])
    vr_ref[0, rows, :] = seg("vr").astype(BF16)


def _inproj(x, mul, shift, w_bf16, cos, sin, ones_blk, qg, kg, *, tm, sub):
    bsz, s, d = x.shape
    n = w_bf16.shape[1]
    nt = s // tm
    const = lambda b, i: (0, 0)
    tok3 = lambda b, i: (b, i, 0)
    head4 = lambda b, i: (b, 0, i, 0)
    out_shape = [
        jax.ShapeDtypeStruct((bsz, ATTN_HEADS, HEAD_DIM, s), BF16),
        jax.ShapeDtypeStruct((bsz, ATTN_KV_HEADS, s, HEAD_DIM), BF16),
        jax.ShapeDtypeStruct((bsz, ATTN_KV_HEADS, VT_ROWS, s), BF16),
        jax.ShapeDtypeStruct((bsz, s, 512), BF16),
        jax.ShapeDtypeStruct((bsz, RET_HEADS, s, HEAD_DIM), BF16),
        jax.ShapeDtypeStruct((bsz, RET_HEADS, HEAD_DIM, s), BF16),
        jax.ShapeDtypeStruct((bsz, s, 512), BF16),
        jax.ShapeDtypeStruct((bsz, s, 512), BF16),
        jax.ShapeDtypeStruct((bsz, s, 2048), BF16),
    ]
    out_specs = [
        pl.BlockSpec((1, ATTN_HEADS, HEAD_DIM, tm), lambda b, i: (b, 0, 0, i)),
        pl.BlockSpec((1, ATTN_KV_HEADS, tm, HEAD_DIM), head4),
        pl.BlockSpec((1, ATTN_KV_HEADS, VT_ROWS, tm), lambda b, i: (b, 0, 0, i)),
        pl.BlockSpec((1, tm, 512), tok3),
        pl.BlockSpec((1, RET_HEADS, tm, HEAD_DIM), head4),
        pl.BlockSpec((1, RET_HEADS, HEAD_DIM, tm), lambda b, i: (b, 0, 0, i)),
        pl.BlockSpec((1, tm, 512), tok3),
        pl.BlockSpec((1, tm, 512), tok3),
        pl.BlockSpec((1, tm, 2048), tok3),
    ]
    return pl.pallas_call(
        functools.partial(_inproj_kernel, tm=tm, sub=sub),
        grid=(bsz, nt),
        in_specs=[pl.BlockSpec((1, tm, d), tok3),
                  pl.BlockSpec((1, 1, d), lambda b, i: (b, 0, 0)),
                  pl.BlockSpec((1, 1, d), lambda b, i: (b, 0, 0)),
                  pl.BlockSpec((d, n), const, pipeline_mode=pl.Buffered(1)),
                  pl.BlockSpec((tm, LANES), lambda b, i: (i, 0)),
                  pl.BlockSpec((tm, LANES), lambda b, i: (i, 0)),
                  pl.BlockSpec((512, 512), const),
                  pl.BlockSpec((1, 512), const),
                  pl.BlockSpec((1, LANES), const)],
        out_specs=out_specs,
        out_shape=out_shape,
        compiler_params=_params("arbitrary", "arbitrary"),
        name="in_proj",
    )(x, mul, shift, w_bf16, cos, sin, ones_blk, qg, kg)


def _attn_kernel(q_ref, k_ref, v_ref, z_ref, o_ref, *, tq, tk, cw, blocks_per_group, seq, lookahead):
    group = ATTN_HEADS // ATTN_KV_HEADS
    cols = ATTN_HEADS * tq

    nblk = cols // cw
    units = [(c, j) for g0 in range(0, nblk, blocks_per_group) for j in range(seq // tk)
             for c in range(g0, g0 + blocks_per_group)]

    def kv_head(c):
        return (c * cw // tq) // group

    def scores(c, j):
        head, start = divmod(c * cw, tq)
        return _mm(k_ref[0, head // group, j * tk:(j + 1) * tk, :], q_ref[0, head, :, start:start + cw])

    m = [jnp.full((1, cw), -jnp.inf, F32)] * (cols // cw)
    acc = [jnp.zeros((VT_ROWS, cw), F32)] * (cols // cw)
    pending = [scores(*u) for u in units[:lookahead]]
    for i, (c, j) in enumerate(units):
        if i + lookahead < len(units):
            pending.append(scores(*units[i + lookahead]))
        st = pending.pop(0)
        m_new = jnp.maximum(m[c], jnp.max(st, axis=0, keepdims=True))
        pt = jnp.exp2(st - m_new).astype(BF16)
        acc[c] = jnp.exp2(m[c] - m_new) * acc[c] + _mm(v_ref[0, kv_head(c), :, j * tk:(j + 1) * tk], pt)
        m[c] = m_new
    acc = jnp.concatenate(acc, axis=1)
    out = acc[:HEAD_DIM] / acc[HEAD_DIM:HEAD_DIM + 1]
    out = jnp.concatenate([out[:, h * tq:(h + 1) * tq].T for h in range(ATTN_HEADS)], axis=1)
    o_ref[0] = (out * z_ref[0].astype(F32)).astype(BF16)


def _attention(qa, ka, va, za, *, tq, tk, cw, blocks_per_group, lookahead):
    bsz, _, _, s = qa.shape
    width = ATTN_HEADS * HEAD_DIM
    return pl.pallas_call(
        functools.partial(_attn_kernel, tq=tq, tk=tk, cw=cw, blocks_per_group=blocks_per_group, seq=s,
                          lookahead=lookahead),
        grid=(bsz, s // tq),
        in_specs=[pl.BlockSpec((1, ATTN_HEADS, HEAD_DIM, tq), lambda b, i: (b, 0, 0, i)),
                  pl.BlockSpec((1, ATTN_KV_HEADS, s, HEAD_DIM), lambda b, i: (b, 0, 0, 0)),
                  pl.BlockSpec((1, ATTN_KV_HEADS, VT_ROWS, s), lambda b, i: (b, 0, 0, 0)),
                  pl.BlockSpec((1, tq, width), lambda b, i: (b, i, 0))],
        out_specs=pl.BlockSpec((1, tq, width), lambda b, i: (b, i, 0)),
        out_shape=jax.ShapeDtypeStruct((bsz, s, width), BF16),
        compiler_params=_params("arbitrary", "arbitrary"),
        name="gqa_attention",
    )(qa, ka, va, za)


def _log_sigmoid(w):
    return -(jnp.maximum(-w, 0.0) + jnp.log1p(jnp.exp(-jnp.abs(w))))


def _ret_kernel(q_ref, k_ref, v_ref, z_ref, wf_ref, wb_ref, gn_ref, o_ref, *, chunk, seq, lookahead):
    nchunks = seq // chunk
    lg_f = _log_sigmoid(wf_ref[0])[0:1, 0:1]
    lg_b = _log_sigmoid(wb_ref[0])[0:1, 0:1]

    row = lax.broadcasted_iota(jnp.int32, (chunk, chunk), 0).astype(F32)
    colm = lax.broadcasted_iota(jnp.int32, (chunk, chunk), 1).astype(F32)
    diff = row - colm
    decay = jnp.where(diff >= 0, jnp.exp(lg_f * jnp.maximum(diff, 0.0)), jnp.exp(lg_b * jnp.maximum(-diff, 0.0)))
    idx = lax.broadcasted_iota(jnp.int32, (chunk, HEAD_DIM), 0).astype(F32)
    q_dec_f = jnp.exp(lg_f * (idx + 1.0))
    q_dec_b = jnp.exp(lg_b * (chunk - idx))
    idx_t = lax.broadcasted_iota(jnp.int32, (HEAD_DIM, chunk), 1).astype(F32)
    k_dec_f = jnp.exp(lg_f * (chunk - 1.0 - idx_t))
    k_dec_b = jnp.exp(lg_b * idx_t)
    chunk_dec_f = jnp.exp(lg_f * chunk)
    chunk_dec_b = jnp.exp(lg_b * chunk)

    def rows(n):
        return slice(n * chunk, (n + 1) * chunk)

    def kv_update(n, k_dec):
        kd = (k_ref[0, 0, :, rows(n)].astype(F32) * k_dec).astype(BF16)
        return _mm(kd, v_ref[0, rows(n), :])

    def scan(order, k_dec, chunk_dec):
        updates = {n: kv_update(n, k_dec) for n in order}
        states, state = {}, jnp.zeros((HEAD_DIM, RET_VALUE_DIM), F32)
        for n in order:
            states[n] = state.astype(BF16)
            state = chunk_dec * state + updates[n]
        return states

    state_b = scan(range(nchunks - 1, -1, -1), k_dec_b, chunk_dec_b)
    state_f = scan(range(nchunks), k_dec_f, chunk_dec_f)

    def scores(n):
        return _mm(q_ref[0, 0, rows(n), :], k_ref[0, 0, :, rows(n)])

    gn = gn_ref[...]
    pending = [scores(n) for n in range(min(lookahead, nchunks))]
    for n in range(nchunks):
        if n + lookahead < nchunks:
            pending.append(scores(n + lookahead))
        qf = q_ref[0, 0, rows(n), :].astype(F32)
        o = _mm((pending.pop(0) * decay).astype(BF16), v_ref[0, rows(n), :])
        o += _mm((qf * q_dec_f).astype(BF16), state_f[n])
        o += _mm((qf * q_dec_b).astype(BF16), state_b[n])
        mu = jnp.mean(o, axis=-1, keepdims=True)
        cen = o - mu
        var = jnp.mean(cen * cen, axis=-1, keepdims=True)
        y = cen * lax.rsqrt(var + EPS) * gn * z_ref[0, rows(n), :].astype(F32)
        o_ref[0, rows(n), :] = y.astype(BF16)


def _retention(qr, kr, vr, zr, wf, wb, gn, *, chunk, lookahead):
    bsz, heads, s, _ = qr.shape
    head_blk = lambda b, h: (b, h, 0, 0)
    col_blk = lambda b, h: (b, 0, h)
    return pl.pallas_call(
        functools.partial(_ret_kernel, chunk=chunk, seq=s, lookahead=lookahead),
        grid=(bsz, heads),
        in_specs=[pl.BlockSpec((1, 1, s, HEAD_DIM), head_blk),
                  pl.BlockSpec((1, 1, HEAD_DIM, s), head_blk),
                  pl.BlockSpec((1, s, RET_VALUE_DIM), col_blk),
                  pl.BlockSpec((1, s, RET_VALUE_DIM), col_blk),
                  pl.BlockSpec((1, 8, LANES), lambda b, h: (h, 0, 0)),
                  pl.BlockSpec((1, 8, LANES), lambda b, h: (h, 0, 0)),
                  pl.BlockSpec((1, RET_VALUE_DIM), lambda b, h: (0, h))],
        out_specs=pl.BlockSpec((1, s, RET_VALUE_DIM), col_blk),
        out_shape=jax.ShapeDtypeStruct((bsz, s, heads * RET_VALUE_DIM), BF16),
        compiler_params=_params("arbitrary", "arbitrary"),
        name="retention",
    )(qr, kr, vr, zr, wf, wb, gn)


def _outproj_kernel(ya_ref, yr_ref, gl_ref, x_ref, gate_ref, wpa_ref, wpr_ref, wout_ref, gpost_ref, o_ref, *, sub):
    d = x_ref.shape[-1]
    tiles = [pl.ds(r, sub) for r in range(0, x_ref.shape[1], sub)]
    branch = [(_mm(ya_ref[0, rows, :], wpa_ref[...]), _mm(yr_ref[0, rows, :], wpr_ref[...])) for rows in tiles]
    for rows, (pa, pr) in zip(tiles, branch):
        merged = gl_ref[0, rows, :d].astype(F32) * pa + gl_ref[0, rows, d:].astype(F32) * pr
        z = _mm(merged.astype(BF16), wout_ref[...])
        ms = jnp.mean(z * z, axis=-1, keepdims=True)
        y = z * lax.rsqrt(ms + EPS) * gpost_ref[...]
        o_ref[0, rows, :] = x_ref[0, rows, :] + gate_ref[0] * y


def _outproj(ya, yr, gl, x, gate, wpa, wpr, wout, gpost, *, tm, sub):
    bsz, s, d = x.shape
    tok3 = lambda b, i: (b, i, 0)
    const = lambda b, i: (0, 0)
    return pl.pallas_call(
        functools.partial(_outproj_kernel, sub=sub),
        grid=(bsz, s // tm),
        in_specs=[pl.BlockSpec((1, tm, ya.shape[-1]), tok3),
                  pl.BlockSpec((1, tm, yr.shape[-1]), tok3),
                  pl.BlockSpec((1, tm, 2 * d), tok3),
                  pl.BlockSpec((1, tm, d), tok3),
                  pl.BlockSpec((1, 1, d), lambda b, i: (b, 0, 0)),
                  pl.BlockSpec(wpa.shape, const),
                  pl.BlockSpec(wpr.shape, const),
                  pl.BlockSpec(wout.shape, const),
                  pl.BlockSpec((1, d), const)],
        out_specs=pl.BlockSpec((1, tm, d), tok3),
        out_shape=jax.ShapeDtypeStruct((bsz, s, d), F32),
        compiler_params=_params("arbitrary", "arbitrary"),
        name="out_proj",
    )(ya, yr, gl, x, gate, wpa, wpr, wout, gpost)


def _rope_tables(seq):
    t = jnp.arange(seq)
    row = (t // GRID_W).astype(F32)
    colp = (t % GRID_W).astype(F32)
    half = HEAD_DIM // 2
    inv_freq = ROPE_THETA ** (-jnp.arange(0, half, 2, dtype=F32) / half)
    ang_r = row[:, None] * inv_freq[None, :]
    ang_c = colp[:, None] * inv_freq[None, :]
    cos = jnp.concatenate([jnp.cos(ang_r)] * 2 + [jnp.cos(ang_c)] * 2, axis=-1)
    sin = jnp.concatenate([-jnp.sin(ang_r), jnp.sin(ang_r), -jnp.sin(ang_c), jnp.sin(ang_c)], axis=-1)
    return jnp.tile(cos, (1, 2)), jnp.tile(sin, (1, 2))


def _layer(x, mod, g_pre, w_in, qn_g, kn_g, w_dec_f, w_dec_b, gn_g, w_pa, w_pr, w_out, g_post, tables):
    bsz, s, d = x.shape
    shift, scale, gate = jnp.split(mod, 3, axis=-1)
    mul = (g_pre[None, :] * (1.0 + scale)).reshape(bsz, 1, d)
    shift = shift.reshape(bsz, 1, d)
    gate = gate.reshape(bsz, 1, d)

    blk = jnp.arange(512) // HEAD_DIM
    ones_blk = (blk[:, None] == blk[None, :]).astype(BF16)
    qg = jnp.tile(qn_g * (HEAD_DIM ** -0.5 * LOG2E), ATTN_HEADS).reshape(1, 512)
    kg = jnp.tile(kn_g, ATTN_KV_HEADS).reshape(1, LANES)
    cos, sin = tables

    qa, ka, va, za, qr, kr, vr, zr, gl = _inproj(x, mul, shift, w_in.astype(BF16), cos, sin, ones_blk, qg, kg, tm=1024, sub=512)
    ya = _attention(qa, ka, va, za, tq=512, tk=256, cw=256, blocks_per_group=2, lookahead=8)
    wf = jnp.broadcast_to(w_dec_f.astype(F32)[:, None, None], (RET_HEADS, 8, LANES))
    wb = jnp.broadcast_to(w_dec_b.astype(F32)[:, None, None], (RET_HEADS, 8, LANES))
    yr = _retention(qr, kr, vr, zr, wf, wb, gn_g.reshape(1, -1), chunk=256, lookahead=3)
    return _outproj(ya, yr, gl, x, gate, w_pa.astype(BF16), w_pr.astype(BF16), w_out.astype(BF16),
                    g_post.reshape(1, d), tm=1024, sub=512)


def kernel(x, c, w_ada, b_ada, g_pre, w_in, qn_g, kn_g, w_dec_f, w_dec_b, gn_g, w_pa, w_pr, w_out, g_post):
    tables = _rope_tables(x.shape[1])
    for l in range(w_ada.shape[0]):
        mod = _ada(c, w_ada[l], b_ada[l])
        x = _layer(x, mod, g_pre[l], w_in[l], qn_g[l], kn_g[l], w_dec_f[l], w_dec_b[l], gn_g[l],
                   w_pa[l], w_pr[l], w_out[l], g_post[l], tables)
    return x
```

```python
import functools
import math

import jax
import jax.numpy as jnp
from jax import lax
from jax.experimental import pallas as pl
from jax.experimental.pallas import tpu as pltpu

GRID_W = 64
ATTN_HEADS = 8
ATTN_KV_HEADS = 2
HEAD_DIM = 64
RET_HEADS = 4
RET_VALUE_DIM = 128
VT_ROWS = 80
ROPE_THETA = 10000.0
EPS = 1e-6

IN_SIZES = {"qa": 512, "ka": 128, "va": 128, "za": 512, "qr": 256, "kr": 256, "vr": 512, "zr": 512, "gl": 2048}
IN_OFFSETS = dict(zip(IN_SIZES, [sum(list(IN_SIZES.values())[:i]) for i in range(len(IN_SIZES))]))

LANES = 128
VMEM_LIMIT_BYTES = 56 * 1024 * 1024

F32 = jnp.float32
BF16 = jnp.bfloat16
LOG2E = math.log2(math.e)


def _mm(a, b):
    return jnp.dot(a, b, preferred_element_type=F32)


def _sigmoid(x):
    return 1.0 / (1.0 + jnp.exp(-x))


def _params(*semantics):
    return pltpu.CompilerParams(dimension_semantics=semantics, vmem_limit_bytes=VMEM_LIMIT_BYTES)


def _ada_kernel(c_ref, w_ref, b_ref, o_ref):
    c = c_ref[...]
    ca = (c * _sigmoid(c)).astype(BF16)
    o_ref[...] = _mm(ca, w_ref[...].astype(BF16)) + b_ref[...]


def _ada(c, w, b):
    bsz, d = c.shape
    n = w.shape[1]
    return pl.pallas_call(
        _ada_kernel,
        grid=(n // d,),
        in_specs=[pl.BlockSpec((bsz, d), lambda j: (0, 0)),
                  pl.BlockSpec((d, d), lambda j: (0, j)),
                  pl.BlockSpec((1, d), lambda j: (0, j))],
        out_specs=pl.BlockSpec((bsz, d), lambda j: (0, j)),
        out_shape=jax.ShapeDtypeStruct((bsz, n), F32),
        compiler_params=_params("arbitrary"),
        name="ada_mod",
    )(c, w, b.reshape(1, n))


def _rope(x, cos, sin, lo):
    partner = jnp.where(lo, pltpu.roll(x, LANES - 16, 1), pltpu.roll(x, 16, 1))
    return x * cos + partner * sin


def _head_rms(x, ones_blk):
    ms = _mm((x * x).astype(BF16), ones_blk) * (1.0 / HEAD_DIM)
    return x * lax.rsqrt(ms + EPS)


def _inproj_kernel(x_ref, mul_ref, shift_ref, w_ref, cos_ref, sin_ref, ones_ref, qg_ref, kg_ref,
                   qa_ref, ka_ref, va_ref, za_ref, qr_ref, kr_ref, vr_ref, zr_ref, gl_ref, *, tm, sub):
    for r0 in range(0, tm, sub):
        _inproj_rows(slice(r0, r0 + sub), x_ref, mul_ref, shift_ref, w_ref, cos_ref, sin_ref, ones_ref, qg_ref, kg_ref,
                     qa_ref, ka_ref, va_ref, za_ref, qr_ref, kr_ref, vr_ref, zr_ref, gl_ref)


def _inproj_rows(rows, x_ref, mul_ref, shift_ref, w_ref, cos_ref, sin_ref, ones_ref, qg_ref, kg_ref,
                 qa_ref, ka_ref, va_ref, za_ref, qr_ref, kr_ref, vr_ref, zr_ref, gl_ref):
    nrows = rows.stop - rows.start
    x = x_ref[0, rows, :]
    ms = jnp.mean(x * x, axis=-1, keepdims=True)
    h = (x * lax.rsqrt(ms + EPS) * mul_ref[0] + shift_ref[0]).astype(BF16)

    cos = cos_ref[rows, :]
    sin = sin_ref[rows, :]
    lane = lax.broadcasted_iota(jnp.int32, (nrows, LANES), 1)
    lo = (lane & 31) < 16
    ones_blk = ones_ref[...]

    def rope_cols(y, scale_row):
        outs = []
        for j in range(y.shape[1] // LANES):
            blk = y[:, j * LANES:(j + 1) * LANES]
            if scale_row is not None:
                blk = blk * scale_row[:, j * LANES:(j + 1) * LANES]
            outs.append(_rope(blk, cos, sin, lo))
        return outs

    def seg(name, part=0, width=None):
        start = IN_OFFSETS[name] + part
        width = IN_SIZES[name] if width is None else width
        return _mm(h, w_ref[:, start:start + width])

    def store_heads(ref, blks):
        for j, blk in enumerate(blks):
            blk = blk.astype(BF16)
            ref[0, 2 * j, rows, :] = blk[:, :HEAD_DIM]
            ref[0, 2 * j + 1, rows, :] = blk[:, HEAD_DIM:]

    def store_heads_transposed(ref, blks):
        for j, blk in enumerate(blks):
            blk_t = blk.T.astype(BF16)
            ref[0, 2 * j, :, rows] = blk_t[:HEAD_DIM]
            ref[0, 2 * j + 1, :, rows] = blk_t[HEAD_DIM:]

    for j in range(IN_SIZES["gl"] // 512):
        gl_ref[0, rows, j * 512:(j + 1) * 512] = _sigmoid(seg("gl", j * 512, 512)).astype(BF16)
    z = seg("za")
    za_ref[0, rows, :] = (z * _sigmoid(z)).astype(BF16)
    z = seg("zr")
    zr_ref[0, rows, :] = (z * _sigmoid(z)).astype(BF16)
    q = _head_rms(seg("qa"), ones_blk)
    store_heads_transposed(qa_ref, rope_cols(q, qg_ref[...]))
    kv = seg("ka", 0, IN_SIZES["ka"] + IN_SIZES["va"])
    k = _head_rms(kv[:, :LANES], ones_blk[:LANES, :LANES])
    store_heads(ka_ref, rope_cols(k, kg_ref[...]))
    vt = kv[:, LANES:].T
    ones_rows = jnp.where(lax.broadcasted_iota(jnp.int32, (VT_ROWS - HEAD_DIM, nrows), 0) == 0, 1.0, 0.0)
    for g in range(ATTN_KV_HEADS):
        ext = jnp.concatenate([vt[g * HEAD_DIM:(g + 1) * HEAD_DIM], ones_rows], axis=0)
        va_ref[0, g, :, rows] = ext.astype(BF16)
    store_heads(qr_ref, rope_cols(seg("qr"), None))
    store_heads_transposed(kr_ref, [blk * HEAD_DIM ** -0.5 for blk in rope_cols(seg("kr"), None)])
    vr_ref[0, rows, :] = seg("vr").astype(BF16)


def _inproj(x, mul, shift, w_bf16, cos, sin, ones_blk, qg, kg, *, tm, sub):
    bsz, s, d = x.shape
    n = w_bf16.shape[1]
    nt = s // tm
    const = lambda b, i: (0, 0)
    tok3 = lambda b, i: (b, i, 0)
    head4 = lambda b, i: (b, 0, i, 0)
    out_shape = [
        jax.ShapeDtypeStruct((bsz, ATTN_HEADS, HEAD_DIM, s), BF16),
        jax.ShapeDtypeStruct((bsz, ATTN_KV_HEADS, s, HEAD_DIM), BF16),
        jax.ShapeDtypeStruct((bsz, ATTN_KV_HEADS, VT_ROWS, s), BF16),
        jax.ShapeDtypeStruct((bsz, s, 512), BF16),
        jax.ShapeDtypeStruct((bsz, RET_HEADS, s, HEAD_DIM), BF16),
        jax.ShapeDtypeStruct((bsz, RET_HEADS, HEAD_DIM, s), BF16),
        jax.ShapeDtypeStruct((bsz, s, 512), BF16),
        jax.ShapeDtypeStruct((bsz, s, 512), BF16),
        jax.ShapeDtypeStruct((bsz, s, 2048), BF16),
    ]
    out_specs = [
        pl.BlockSpec((1, ATTN_HEADS, HEAD_DIM, tm), lambda b, i: (b, 0, 0, i)),
        pl.BlockSpec((1, ATTN_KV_HEADS, tm, HEAD_DIM), head4),
        pl.BlockSpec((1, ATTN_KV_HEADS, VT_ROWS, tm), lambda b, i: (b, 0, 0, i)),
        pl.BlockSpec((1, tm, 512), tok3),
        pl.BlockSpec((1, RET_HEADS, tm, HEAD_DIM), head4),
        pl.BlockSpec((1, RET_HEADS, HEAD_DIM, tm), lambda b, i: (b, 0, 0, i)),
        pl.BlockSpec((1, tm, 512), tok3),
        pl.BlockSpec((1, tm, 512), tok3),
        pl.BlockSpec((1, tm, 2048), tok3),
    ]
    return pl.pallas_call(
        functools.partial(_inproj_kernel, tm=tm, sub=sub),
        grid=(bsz, nt),
        in_specs=[pl.BlockSpec((1, tm, d), tok3),
                  pl.BlockSpec((1, 1, d), lambda b, i: (b, 0, 0)),
                  pl.BlockSpec((1, 1, d), lambda b, i: (b, 0, 0)),
                  pl.BlockSpec((d, n), const, pipeline_mode=pl.Buffered(1)),
                  pl.BlockSpec((tm, LANES), lambda b, i: (i, 0)),
                  pl.BlockSpec((tm, LANES), lambda b, i: (i, 0)),
                  pl.BlockSpec((512, 512), const),
                  pl.BlockSpec((1, 512), const),
                  pl.BlockSpec((1, LANES), const)],
        out_specs=out_specs,
        out_shape=out_shape,
        compiler_params=_params("arbitrary", "arbitrary"),
        name="in_proj",
    )(x, mul, shift, w_bf16, cos, sin, ones_blk, qg, kg)


def _attn_kernel(q_ref, k_ref, v_ref, z_ref, o_ref, *, tq, tk, cw, blocks_per_group, seq, lookahead):
    group = ATTN_HEADS // ATTN_KV_HEADS
    cols = ATTN_HEADS * tq

    nblk = cols // cw
    units = [(c, j) for g0 in range(0, nblk, blocks_per_group) for j in range(seq // tk)
             for c in range(g0, g0 + blocks_per_group)]

    def kv_head(c):
        return (c * cw // tq) // group

    def scores(c, j):
        head, start = divmod(c * cw, tq)
        return _mm(k_ref[0, head // group, j * tk:(j + 1) * tk, :], q_ref[0, head, :, start:start + cw])

    m = [jnp.full((1, cw), -jnp.inf, F32)] * (cols // cw)
    acc = [jnp.zeros((VT_ROWS, cw), F32)] * (cols // cw)
    pending = [scores(*u) for u in units[:lookahead]]
    for i, (c, j) in enumerate(units):
        if i + lookahead < len(units):
            pending.append(scores(*units[i + lookahead]))
        st = pending.pop(0)
        m_new = jnp.maximum(m[c], jnp.max(st, axis=0, keepdims=True))
        pt = jnp.exp2(st - m_new).astype(BF16)
        acc[c] = jnp.exp2(m[c] - m_new) * acc[c] + _mm(v_ref[0, kv_head(c), :, j * tk:(j + 1) * tk], pt)
        m[c] = m_new
    acc = jnp.concatenate(acc, axis=1)
    out = acc[:HEAD_DIM] / acc[HEAD_DIM:HEAD_DIM + 1]
    out = jnp.concatenate([out[:, h * tq:(h + 1) * tq].T for h in range(ATTN_HEADS)], axis=1)
    o_ref[0] = (out * z_ref[0].astype(F32)).astype(BF16)


def _attention(qa, ka, va, za, *, tq, tk, cw, blocks_per_group, lookahead):
    bsz, _, _, s = qa.shape
    width = ATTN_HEADS * HEAD_DIM
    return pl.pallas_call(
        functools.partial(_attn_kernel, tq=tq, tk=tk, cw=cw, blocks_per_group=blocks_per_group, seq=s,
                          lookahead=lookahead),
        grid=(bsz, s // tq),
        in_specs=[pl.BlockSpec((1, ATTN_HEADS, HEAD_DIM, tq), lambda b, i: (b, 0, 0, i)),
                  pl.BlockSpec((1, ATTN_KV_HEADS, s, HEAD_DIM), lambda b, i: (b, 0, 0, 0)),
                  pl.BlockSpec((1, ATTN_KV_HEADS, VT_ROWS, s), lambda b, i: (b, 0, 0, 0)),
                  pl.BlockSpec((1, tq, width), lambda b, i: (b, i, 0))],
        out_specs=pl.BlockSpec((1, tq, width), lambda b, i: (b, i, 0)),
        out_shape=jax.ShapeDtypeStruct((bsz, s, width), BF16),
        compiler_params=_params("arbitrary", "arbitrary"),
        name="gqa_attention",
    )(qa, ka, va, za)


def _log_sigmoid(w):
    return -(jnp.maximum(-w, 0.0) + jnp.log1p(jnp.exp(-jnp.abs(w))))


def _ret_kernel(q_ref, k_ref, v_ref, z_ref, wf_ref, wb_ref, gn_ref, o_ref, *, chunk, seq, lookahead):
    nchunks = seq // chunk
    lg_f = _log_sigmoid(wf_ref[0])[0:1, 0:1]
    lg_b = _log_sigmoid(wb_ref[0])[0:1, 0:1]

    row = lax.broadcasted_iota(jnp.int32, (chunk, chunk), 0).astype(F32)
    colm = lax.broadcasted_iota(jnp.int32, (chunk, chunk), 1).astype(F32)
    diff = row - colm
    decay = jnp.where(diff >= 0, jnp.exp(lg_f * jnp.maximum(diff, 0.0)), jnp.exp(lg_b * jnp.maximum(-diff, 0.0)))
    idx = lax.broadcasted_iota(jnp.int32, (chunk, HEAD_DIM), 0).astype(F32)
    q_dec_f = jnp.exp(lg_f * (idx + 1.0))
    q_dec_b = jnp.exp(lg_b * (chunk - idx))
    idx_t = lax.broadcasted_iota(jnp.int32, (HEAD_DIM, chunk), 1).astype(F32)
    k_dec_f = jnp.exp(lg_f * (chunk - 1.0 - idx_t))
    k_dec_b = jnp.exp(lg_b * idx_t)
    chunk_dec_f = jnp.exp(lg_f * chunk)
    chunk_dec_b = jnp.exp(lg_b * chunk)

    def rows(n):
        return slice(n * chunk, (n + 1) * chunk)

    def kv_update(n, k_dec):
        kd = (k_ref[0, 0, :, rows(n)].astype(F32) * k_dec).astype(BF16)
        return _mm(kd, v_ref[0, rows(n), :])

    def scan(order, k_dec, chunk_dec):
        updates = {n: kv_update(n, k_dec) for n in order}
        states, state = {}, jnp.zeros((HEAD_DIM, RET_VALUE_DIM), F32)
        for n in order:
            states[n] = state.astype(BF16)
            state = chunk_dec * state + updates[n]
        return states

    state_b = scan(range(nchunks - 1, -1, -1), k_dec_b, chunk_dec_b)
    state_f = scan(range(nchunks), k_dec_f, chunk_dec_f)

    def scores(n):
        return _mm(q_ref[0, 0, rows(n), :], k_ref[0, 0, :, rows(n)])

    gn = gn_ref[...]
    pending = [scores(n) for n in range(min(lookahead, nchunks))]
    for n in range(nchunks):
        if n + lookahead < nchunks:
            pending.append(scores(n + lookahead))
        qf = q_ref[0, 0, rows(n), :].astype(F32)
        o = _mm((pending.pop(0) * decay).astype(BF16), v_ref[0, rows(n), :])
        o += _mm((qf * q_dec_f).astype(BF16), state_f[n])
        o += _mm((qf * q_dec_b).astype(BF16), state_b[n])
        mu = jnp.mean(o, axis=-1, keepdims=True)
        cen = o - mu
        var = jnp.mean(cen * cen, axis=-1, keepdims=True)
        y = cen * lax.rsqrt(var + EPS) * gn * z_ref[0, rows(n), :].astype(F32)
        o_ref[0, rows(n), :] = y.astype(BF16)


def _retention(qr, kr, vr, zr, wf, wb, gn, *, chunk, lookahead):
    bsz, heads, s, _ = qr.shape
    head_blk = lambda b, h: (b, h, 0, 0)
    col_blk = lambda b, h: (b, 0, h)
    return pl.pallas_call(
        functools.partial(_ret_kernel, chunk=chunk, seq=s, lookahead=lookahead),
        grid=(bsz, heads),
        in_specs=[pl.BlockSpec((1, 1, s, HEAD_DIM), head_blk),
                  pl.BlockSpec((1, 1, HEAD_DIM, s), head_blk),
                  pl.BlockSpec((1, s, RET_VALUE_DIM), col_blk),
                  pl.BlockSpec((1, s, RET_VALUE_DIM), col_blk),
                  pl.BlockSpec((1, 8, LANES), lambda b, h: (h, 0, 0)),
                  pl.BlockSpec((1, 8, LANES), lambda b, h: (h, 0, 0)),
                  pl.BlockSpec((1, RET_VALUE_DIM), lambda b, h: (0, h))],
        out_specs=pl.BlockSpec((1, s, RET_VALUE_DIM), col_blk),
        out_shape=jax.ShapeDtypeStruct((bsz, s, heads * RET_VALUE_DIM), BF16),
        compiler_params=_params("arbitrary", "arbitrary"),
        name="retention",
    )(qr, kr, vr, zr, wf, wb, gn)


def _outproj_kernel(ya_ref, yr_ref, gl_ref, x_ref, gate_ref, wpa_ref, wpr_ref, wout_ref, gpost_ref, o_ref, *, sub):
    d = x_ref.shape[-1]
    tiles = [pl.ds(r, sub) for r in range(0, x_ref.shape[1], sub)]
    branch = [(_mm(ya_ref[0, rows, :], wpa_ref[...]), _mm(yr_ref[0, rows, :], wpr_ref[...])) for rows in tiles]
    for rows, (pa, pr) in zip(tiles, branch):
        merged = gl_ref[0, rows, :d].astype(F32) * pa + gl_ref[0, rows, d:].astype(F32) * pr
        z = _mm(merged.astype(BF16), wout_ref[...])
        ms = jnp.mean(z * z, axis=-1, keepdims=True)
        y = z * lax.rsqrt(ms + EPS) * gpost_ref[...]
        o_ref[0, rows, :] = x_ref[0, rows, :] + gate_ref[0] * y


def _outproj(ya, yr, gl, x, gate, wpa, wpr, wout, gpost, *, tm, sub):
    bsz, s, d = x.shape
    tok3 = lambda b, i: (b, i, 0)
    const = lambda b, i: (0, 0)
    return pl.pallas_call(
        functools.partial(_outproj_kernel, sub=sub),
        grid=(bsz, s // tm),
        in_specs=[pl.BlockSpec((1, tm, ya.shape[-1]), tok3),
                  pl.BlockSpec((1, tm, yr.shape[-1]), tok3),
                  pl.BlockSpec((1, tm, 2 * d), tok3),
                  pl.BlockSpec((1, tm, d), tok3),
                  pl.BlockSpec((1, 1, d), lambda b, i: (b, 0, 0)),
                  pl.BlockSpec(wpa.shape, const),
                  pl.BlockSpec(wpr.shape, const),
                  pl.BlockSpec(wout.shape, const),
                  pl.BlockSpec((1, d), const)],
        out_specs=pl.BlockSpec((1, tm, d), tok3),
        out_shape=jax.ShapeDtypeStruct((bsz, s, d), F32),
        compiler_params=_params("arbitrary", "arbitrary"),
        name="out_proj",
    )(ya, yr, gl, x, gate, wpa, wpr, wout, gpost)


def _rope_tables(seq):
    t = jnp.arange(seq)
    row = (t // GRID_W).astype(F32)
    colp = (t % GRID_W).astype(F32)
    half = HEAD_DIM // 2
    inv_freq = ROPE_THETA ** (-jnp.arange(0, half, 2, dtype=F32) / half)
    ang_r = row[:, None] * inv_freq[None, :]
    ang_c = colp[:, None] * inv_freq[None, :]
    cos = jnp.concatenate([jnp.cos(ang_r)] * 2 + [jnp.cos(ang_c)] * 2, axis=-1)
    sin = jnp.concatenate([-jnp.sin(ang_r), jnp.sin(ang_r), -jnp.sin(ang_c), jnp.sin(ang_c)], axis=-1)
    return jnp.tile(cos, (1, 2)), jnp.tile(sin, (1, 2))


def _layer(x, mod, g_pre, w_in, qn_g, kn_g, w_dec_f, w_dec_b, gn_g, w_pa, w_pr, w_out, g_post, tables):
    bsz, s, d = x.shape
    shift, scale, gate = jnp.split(mod, 3, axis=-1)
    mul = (g_pre[None, :] * (1.0 + scale)).reshape(bsz, 1, d)
    shift = shift.reshape(bsz, 1, d)
    gate = gate.reshape(bsz, 1, d)

    blk = jnp.arange(512) // HEAD_DIM
    ones_blk = (blk[:, None] == blk[None, :]).astype(BF16)
    qg = jnp.tile(qn_g * (HEAD_DIM ** -0.5 * LOG2E), ATTN_HEADS).reshape(1, 512)
    kg = jnp.tile(kn_g, ATTN_KV_HEADS).reshape(1, LANES)
    cos, sin = tables

    qa, ka, va, za, qr, kr, vr, zr, gl = _inproj(x, mul, shift, w_in.astype(BF16), cos, sin, ones_blk, qg, kg, tm=1024, sub=512)
    ya = _attention(qa, ka, va, za, tq=512, tk=256, cw=256, blocks_per_group=2, lookahead=12)
    wf = jnp.broadcast_to(w_dec_f.astype(F32)[:, None, None], (RET_HEADS, 8, LANES))
    wb = jnp.broadcast_to(w_dec_b.astype(F32)[:, None, None], (RET_HEADS, 8, LANES))
    yr = _retention(qr, kr, vr, zr, wf, wb, gn_g.reshape(1, -1), chunk=256, lookahead=3)
    return _outproj(ya, yr, gl, x, gate, w_pa.astype(BF16), w_pr.astype(BF16), w_out.astype(BF16),
                    g_post.reshape(1, d), tm=1024, sub=512)


def kernel(x, c, w_ada, b_ada, g_pre, w_in, qn_g, kn_g, w_dec_f, w_dec_b, gn_g, w_pa, w_pr, w_out, g_post):
    tables = _rope_tables(x.shape[1])
    for l in range(w_ada.shape[0]):
        mod = _ada(c, w_ada[l], b_ada[l])
        x = _layer(x, mod, g_pre[l], w_in[l], qn_g[l], kn_g[l], w_dec_f[l], w_dec_b[l], gn_g[l],
                   w_pa[l], w_pr[l], w_out[l], g_post[l], tables)
    return x
```
